```python
import math
import jax, jax.numpy as jnp
from jax import lax
import numpy as np

D_MODEL = 1024
BATCH = 8
SEQ = 2048
DEPTH = 2

CTX_LEN = 256
GRID_W = 64
D_MIX = D_MODEL
GROUP_WIDTH = D_MIX // 4
HEAD_DIM = 64
GQA_HEADS = GROUP_WIDTH // HEAD_DIM
GQA_KV_HEADS = 2
GQA_GROUP = GQA_HEADS // GQA_KV_HEADS
DIFF_HEADS = GROUP_WIDTH // HEAD_DIM
DIFF_QK_DIM = HEAD_DIM // 2
POOL_WINDOWS = (2, 4, 8, 16)
POOL_GROUP_DIM = GROUP_WIDTH // len(POOL_WINDOWS)
RET_HEADS = GROUP_WIDTH // HEAD_DIM
RET_CHUNK = 128
D_FF = 2816
FFN_RESIDUAL = 0.5
Q_BLOCK = 128
ROPE_BASE = 10000.0
EPS = 1e-6
IN_SPLITS = (GQA_HEADS * HEAD_DIM, GQA_KV_HEADS * HEAD_DIM, GQA_KV_HEADS * HEAD_DIM,
             DIFF_HEADS * HEAD_DIM, DIFF_HEADS * HEAD_DIM, DIFF_HEADS * HEAD_DIM,
             GROUP_WIDTH,
             RET_HEADS * HEAD_DIM, RET_HEADS * HEAD_DIM, RET_HEADS * HEAD_DIM, GROUP_WIDTH)
D_IN = sum(IN_SPLITS)
IN_SPLIT_POINTS = tuple(int(s) for s in np.cumsum(IN_SPLITS)[:-1])

kernel_name = "hybrid_parallel_head_dit_block"

F32 = jnp.float32


def rms_norm(x, g):
    xf = x.astype(F32)
    y = xf * lax.rsqrt(jnp.mean(xf * xf, axis=-1, keepdims=True) + EPS)
    return (y * g.astype(F32)).astype(x.dtype)


def modulate(x, g, shift, scale):
    return rms_norm(x, g) * (1 + scale) + shift


def swiglu(h, wg, wu, wd):
    return (jax.nn.silu(h @ wg) * (h @ wu)) @ wd


def split_heads(t, n):
    b, l, _ = t.shape
    return t.reshape(b, l, n, -1).transpose(0, 2, 1, 3)


def merge_heads(t):
    b, n, l, d = t.shape
    return t.transpose(0, 2, 1, 3).reshape(b, l, n * d)


def axial_rope_tables(n, dim):
    n_rows = n // GRID_W
    rows = jnp.repeat(jnp.arange(n_rows), GRID_W).astype(F32)
    cols = jnp.tile(jnp.arange(GRID_W), n_rows).astype(F32)
    quarter = dim // 4
    inv = ROPE_BASE ** (-jnp.arange(quarter, dtype=F32) / quarter)
    ang = jnp.concatenate([rows[:, None] * inv, cols[:, None] * inv], axis=-1)
    return jnp.cos(ang), jnp.sin(ang)


def apply_axial_rope(x, cos, sin):
    shp = x.shape
    q = shp[-1] // 4
    xr = x.astype(F32).reshape(shp[:-1] + (2, 2, q))
    x1, x2 = xr[..., 0, :], xr[..., 1, :]
    c = cos.reshape(-1, 2, q)
    s = sin.reshape(-1, 2, q)
    out = jnp.stack([x1 * c - x2 * s, x2 * c + x1 * s], axis=-2)
    return out.reshape(shp).astype(x.dtype)


def blocked_attention(qs, ks, coefs, v, scale):
    b, hk, g, l, _ = qs[0].shape
    nb = l // Q_BLOCK
    qb = tuple(jnp.moveaxis(q.reshape(b, hk, g, nb, Q_BLOCK, q.shape[-1]), 3, 0) for q in qs)

    def one_block(qblk):
        acc = 0.0
        for qi, ki, ci in zip(qblk, ks, coefs):
            s = jnp.einsum('bkgqd,bksd->bkgqs', qi, ki, preferred_element_type=F32) * scale
            acc = acc + ci * jax.nn.softmax(s, axis=-1)
        return jnp.einsum('bkgqs,bkse->bkgqe', acc.astype(v.dtype), v)

    out = lax.map(one_block, qb)
    out = jnp.moveaxis(out, 0, 3).reshape(b, hk * g, l, v.shape[-1])
    return merge_heads(out)


def retention_chunkwise(q, k, v, log_g, s0):
    b, h, l, dk = q.shape
    dv = v.shape[-1]
    c = RET_CHUNK
    nc = l // c
    qc = q.reshape(b, h, nc, c, dk)
    kc = k.reshape(b, h, nc, c, dk)
    vc = v.reshape(b, h, nc, c, dv)
    pos = jnp.arange(c, dtype=F32)
    lg = log_g[:, None]
    diff = pos[:, None] - pos[None, :]
    decay_intra = jnp.where(diff >= 0, jnp.exp(jnp.maximum(diff, 0.0) * log_g[:, None, None]), 0.0)
    scores = jnp.einsum('bhncd,bhnsd->bhncs', qc, kc) * decay_intra[:, None]
    intra = jnp.einsum('bhncs,bhnse->bhnce', scores, vc)
    k_dec = kc * jnp.exp((c - 1 - pos) * lg)[:, None, :, None]
    kv = jnp.einsum('bhncd,bhnce->nbhde', k_dec, vc)
    chunk_decay = jnp.exp(c * log_g)[:, None, None]

    def step(s, kv_n):
        return s * chunk_decay + kv_n, s

    _, s_before = lax.scan(step, s0, kv)
    q_dec = qc * jnp.exp((pos + 1) * lg)[:, None, :, None]
    cross = jnp.einsum('bhncd,nbhde->bhnce', q_dec, s_before)
    return (intra + cross).reshape(b, h, l, dv)


def retention_state(k, v, log_g):
    l = k.shape[2]
    w = jnp.exp((l - 1 - jnp.arange(l, dtype=F32))[None, :] * log_g[:, None])
    return jnp.einsum('bhld,bhle,hl->bhde', k, v, w)


def flip_seq(t):
    return jnp.flip(t, axis=2)


def bidir_retention(q, k, v, log_g2, s_f, s_b):
    fwd = retention_chunkwise(q, k, v, log_g2[0], s_f)
    bwd = flip_seq(retention_chunkwise(flip_seq(q), flip_seq(k), flip_seq(v), log_g2[1], s_b))
    return fwd + bwd


def multiscale_pool(u, pool_w, pool_scale):
    b, l, _ = u.shape
    ng = len(POOL_WINDOWS)
    uf = u.astype(F32).reshape(b, l, ng, POOL_GROUP_DIM)
    cs = jnp.pad(jnp.cumsum(uf, axis=1), ((0, 0), (1, 0), (0, 0), (0, 0)))
    t = jnp.arange(l)
    outs = []
    for gi, w in enumerate(POOL_WINDOWS):
        back = w // 2
        ahead = w - 1 - back
        csg = jnp.pad(cs[:, :, gi], ((0, 0), (back, ahead), (0, 0)), mode='edge')
        wsum = csg[:, w:w + l] - csg[:, :l]
        count = (jnp.minimum(t + ahead, l - 1) - jnp.maximum(t - back, 0) + 1).astype(F32)
        outs.append(wsum / count[None, :, None] - uf[:, :, gi])
    pooled = jnp.stack(outs, axis=2)
    y = jnp.einsum('blgc,gcd->blgd', pooled, pool_w.astype(F32))
    return (y.reshape(b, l, -1) * pool_scale.astype(F32)).astype(u.dtype)


def gqa_group(ctx_qkv, lat_qkv, qk_gain, rope, with_ctx):
    def prep(q, k, v):
        q = rms_norm(split_heads(q, GQA_HEADS), qk_gain[0])
        k = rms_norm(split_heads(k, GQA_KV_HEADS), qk_gain[1])
        return q, k, split_heads(v, GQA_KV_HEADS)

    cq, ck, cv = prep(*ctx_qkv)
    lq, lk, lv = prep(*lat_qkv)
    lq = apply_axial_rope(lq, *rope)
    lk = apply_axial_rope(lk, *rope)

    def grp(q):
        return q.reshape(q.shape[0], GQA_KV_HEADS, GQA_GROUP, q.shape[2], HEAD_DIM)

    scale = HEAD_DIM ** -0.5
    y_lat = blocked_attention((grp(lq),), (jnp.concatenate([ck, lk], axis=2),), (1.0,),
                              jnp.concatenate([cv, lv], axis=2), scale)
    y_ctx = blocked_attention((grp(cq),), (ck,), (1.0,), cv, scale) if with_ctx else None
    return y_ctx, y_lat


def diff_group(ctx_qkv, lat_qkv, lam_params, out_gain, rope, layer_idx, with_ctx):
    lam_init = 0.8 - 0.6 * math.exp(-0.3 * layer_idx)
    lp = lam_params.astype(F32)
    lam = jnp.exp(jnp.sum(lp[0] * lp[1])) - jnp.exp(jnp.sum(lp[2] * lp[3])) + lam_init

    def prep(q, k, v):
        q = split_heads(q, DIFF_HEADS)
        k = split_heads(k, DIFF_HEADS)
        return (q[..., :DIFF_QK_DIM], q[..., DIFF_QK_DIM:], k[..., :DIFF_QK_DIM], k[..., DIFF_QK_DIM:],
                split_heads(v, DIFF_HEADS))

    cq1, cq2, ck1, ck2, cv = prep(*ctx_qkv)
    lq1, lq2, lk1, lk2, lv = prep(*lat_qkv)
    lq1, lq2, lk1, lk2 = (apply_axial_rope(t, *rope) for t in (lq1, lq2, lk1, lk2))
    scale = DIFF_QK_DIM ** -0.5
    coefs = (1.0, -lam)

    def finish(y):
        b, l, _ = y.shape
        y = rms_norm(y.reshape(b, l, DIFF_HEADS, HEAD_DIM), out_gain) * (1.0 - lam_init)
        return y.reshape(b, l, -1)

    def cat(a, b):
        return jnp.concatenate([a, b], axis=2)

    y_lat = finish(blocked_attention((lq1[:, :, None], lq2[:, :, None]), (cat(ck1, lk1), cat(ck2, lk2)),
                                     coefs, cat(cv, lv), scale))
    y_ctx = finish(blocked_attention((cq1[:, :, None], cq2[:, :, None]), (ck1, ck2), coefs, cv, scale)) \
        if with_ctx else None
    return y_ctx, y_lat


def retention_group(ctx_qkvg, lat_qkvg, decay_logit, out_gain, with_ctx):
    log_g = jax.nn.log_sigmoid(decay_logit.astype(F32))

    def prep(q, k, v):
        f = lambda t: split_heads(t, RET_HEADS).astype(F32)
        return f(q), f(k) * HEAD_DIM ** -0.5, f(v)

    cq, ck, cv = prep(*ctx_qkvg[:3])
    lq, lk, lv = prep(*lat_qkvg[:3])
    s_f = retention_state(ck, cv, log_g[0])
    s_b = retention_state(flip_seq(ck), flip_seq(cv), log_g[1])

    def finish(o, g):
        o = rms_norm(o, out_gain)
        return (merge_heads(o) * jax.nn.silu(g.astype(F32))).astype(g.dtype)

    y_lat = finish(bidir_retention(lq, lk, lv, log_g, s_f, s_b), lat_qkvg[3])
    y_ctx = None
    if with_ctx:
        zeros = jnp.zeros_like(s_f)
        y_ctx = finish(bidir_retention(cq, ck, cv, log_g, zeros, zeros), ctx_qkvg[3])
    return y_ctx, y_lat


def token_mixers(u_ctx, u_lat, rope_a, rope_b, qk_gain, lam_params, diff_gain, pool_w, pool_scale,
                 decay_logit, ret_gain, layer_idx, with_ctx):
    c_parts = jnp.split(u_ctx, IN_SPLIT_POINTS, axis=-1)
    l_parts = jnp.split(u_lat, IN_SPLIT_POINTS, axis=-1)
    a_ctx, a_lat = gqa_group(c_parts[0:3], l_parts[0:3], qk_gain, rope_a, with_ctx)
    b_ctx, b_lat = diff_group(c_parts[3:6], l_parts[3:6], lam_params, diff_gain, rope_b, layer_idx, with_ctx)
    c_lat = multiscale_pool(l_parts[6], pool_w, pool_scale)
    d_ctx, d_lat = retention_group(c_parts[7:11], l_parts[7:11], decay_logit, ret_gain, with_ctx)
    y_lat = jnp.concatenate([a_lat, b_lat, c_lat, d_lat], axis=-1)
    y_ctx = None
    if with_ctx:
        c_ctx_pool = multiscale_pool(c_parts[6], pool_w, pool_scale)
        y_ctx = jnp.concatenate([a_ctx, b_ctx, c_ctx_pool, d_ctx], axis=-1)
    return y_ctx, y_lat


def ffn_sublayer(x, m, sub, g_pre, g_post, wg, wu, wd):
    h = modulate(x, g_pre, m[:, :, sub, 0], m[:, :, sub, 1])
    return x + FFN_RESIDUAL * m[:, :, sub, 2] * rms_norm(swiglu(h, wg, wu, wd), g_post)


def setup_inputs(seed: int = 0) -> dict:
    key = jax.random.key(seed)
    ks = jax.random.split(key, 20)
    nrm = jax.random.normal
    h = jnp.arange(RET_HEADS, dtype=F32)
    gamma = 1.0 - 2.0 ** (-5.0 - h)
    decay_logit0 = jnp.log(gamma) - jnp.log1p(-gamma)
    return {
        "x": nrm(ks[0], (BATCH, SEQ, D_MODEL), F32),
        "c": nrm(ks[1], (BATCH, D_MODEL), F32),
        "ctx": nrm(ks[2], (BATCH, CTX_LEN, D_MODEL), F32),
        "c_ctx": nrm(ks[3], (D_MODEL,), F32),
        "w_mod": nrm(ks[4], (DEPTH, D_MODEL, 9 * D_MODEL), F32) * (0.3 * D_MODEL ** -0.5),
        "b_mod": nrm(ks[5], (DEPTH, 9 * D_MODEL), F32) * 0.02,
        "norm_gain": 1.0 + 0.02 * nrm(ks[6], (DEPTH, 6, D_MODEL), F32),
        "ffn_w_gate": nrm(ks[7], (DEPTH, 2, D_MODEL, D_FF), F32) * D_MODEL ** -0.5,
        "ffn_w_up": nrm(ks[8], (DEPTH, 2, D_MODEL, D_FF), F32) * D_MODEL ** -0.5,
        "ffn_w_down": nrm(ks[9], (DEPTH, 2, D_FF, D_MODEL), F32) * D_FF ** -0.5,
        "w_in": nrm(ks[10], (DEPTH, D_MODEL, D_IN), F32) * D_MODEL ** -0.5,
        "w_out": nrm(ks[11], (DEPTH, D_MIX, D_MODEL), F32) * D_MIX ** -0.5,
        "attn_qk_gain": 1.0 + 0.02 * nrm(ks[12], (DEPTH, 2, HEAD_DIM), F32),
        "diff_lambda": 0.1 * nrm(ks[13], (DEPTH, 4, DIFF_QK_DIM), F32),
        "diff_out_gain": 1.0 + 0.02 * nrm(ks[14], (DEPTH, HEAD_DIM), F32),
        "pool_w": nrm(ks[15], (DEPTH, len(POOL_WINDOWS), POOL_GROUP_DIM, POOL_GROUP_DIM), F32) * POOL_GROUP_DIM ** -0.5,
        "pool_scale": 1.0 + 0.02 * nrm(ks[16], (DEPTH, GROUP_WIDTH), F32),
        "ret_decay_logit": decay_logit0 + 0.1 * nrm(ks[17], (DEPTH, 2, RET_HEADS), F32),
        "ret_out_gain": 1.0 + 0.02 * nrm(ks[18], (DEPTH, HEAD_DIM), F32),
    }


def reference(x, c, ctx, c_ctx, w_mod, b_mod, norm_gain, ffn_w_gate, ffn_w_up, ffn_w_down, w_in, w_out,
              attn_qk_gain, diff_lambda, diff_out_gain, pool_w, pool_scale, ret_decay_logit, ret_out_gain):
    b, n, _ = x.shape
    rope_a = axial_rope_tables(n, HEAD_DIM)
    rope_b = axial_rope_tables(n, DIFF_QK_DIM)
    silu_c = jax.nn.silu(c)
    silu_cc = jax.nn.silu(c_ctx)
    h_lat, h_ctx = x, ctx
    for i in range(DEPTH):
        last = i == DEPTH - 1
        m_lat = (silu_c @ w_mod[i] + b_mod[i]).reshape(b, 1, 3, 3, D_MODEL).astype(x.dtype)
        m_ctx = (silu_cc @ w_mod[i] + b_mod[i]).reshape(1, 1, 3, 3, D_MODEL).astype(x.dtype)
        g = norm_gain[i]
        h_ctx = ffn_sublayer(h_ctx, m_ctx, 0, g[0], g[1], ffn_w_gate[i, 0], ffn_w_up[i, 0], ffn_w_down[i, 0])
        h_lat = ffn_sublayer(h_lat, m_lat, 0, g[0], g[1], ffn_w_gate[i, 0], ffn_w_up[i, 0], ffn_w_down[i, 0])
        u_ctx = modulate(h_ctx, g[2], m_ctx[:, :, 1, 0], m_ctx[:, :, 1, 1]) @ w_in[i]
        u_lat = modulate(h_lat, g[2], m_lat[:, :, 1, 0], m_lat[:, :, 1, 1]) @ w_in[i]
        y_ctx, y_lat = token_mixers(u_ctx, u_lat, rope_a, rope_b, attn_qk_gain[i], diff_lambda[i],
                                    diff_out_gain[i], pool_w[i], pool_scale[i], ret_decay_logit[i],
                                    ret_out_gain[i], i, not last)
        h_lat = h_lat + m_lat[:, :, 1, 2] * rms_norm(y_lat @ w_out[i], g[3])
        h_lat = ffn_sublayer(h_lat, m_lat, 2, g[4], g[5], ffn_w_gate[i, 1], ffn_w_up[i, 1], ffn_w_down[i, 1])
        if not last:
            h_ctx = h_ctx + m_ctx[:, :, 1, 2] * rms_norm(y_ctx @ w_out[i], g[3])
            h_ctx = ffn_sublayer(h_ctx, m_ctx, 2, g[4], g[5], ffn_w_gate[i, 1], ffn_w_up[i, 1], ffn_w_down[i, 1])
    return h_lat
```

```python
import functools
import math

import jax
import jax.numpy as jnp
from jax import lax
from jax.experimental import pallas as pl
from jax.experimental.pallas import tpu as pltpu

F32 = jnp.float32
BF16 = jnp.bfloat16

D_MODEL = 1024
BATCH = 8
SEQ = 2048
DEPTH = 2
CTX_LEN = 256
GRID_W = 64
HEAD_DIM = 64
DIFF_QK_DIM = 32
GROUP_WIDTH = 256
D_FF = 2816
D_IN = 2560
FFN_RESIDUAL = 0.5
ROPE_BASE = 10000.0
EPS = 1e-6

T_LAT = BATCH * SEQ
T_CTX = BATCH * CTX_LEN
T_ALL = T_LAT + T_CTX
MOD_ROWS = 16

FF_CHUNK = 256
TM_FFN = 1024
TM_PROJ = 512
TQ = 256
RET_C = 256
VMEM_LIMIT = 56 * 1024 * 1024


def _cparams(sem):
    return pltpu.CompilerParams(dimension_semantics=sem, vmem_limit_bytes=VMEM_LIMIT)


def _silu(x):
    return x * jax.nn.sigmoid(x)


def _dot(a, b):
    return jnp.dot(a, b, preferred_element_type=F32)


def _dot_nt(a, b):
    return lax.dot_general(a, b, (((1,), (1,)), ((), ())), preferred_element_type=F32)


def _dot_tn(a, b):
    return lax.dot_general(a, b, (((0,), (0,)), ((), ())), preferred_element_type=F32)


def _block_ones(n, blk):
    r = lax.broadcasted_iota(jnp.int32, (n, n), 0) // blk
    c = lax.broadcasted_iota(jnp.int32, (n, n), 1) // blk
    return r == c


def _group_sums(sq, ones_bf16):
    hi = sq.astype(BF16)
    lo = (sq - hi.astype(F32)).astype(BF16)
    return _dot(hi, ones_bf16) + _dot(lo, ones_bf16)


def _rms_rows(x):
    return lax.rsqrt(jnp.mean(x * x, axis=-1, keepdims=True) + EPS)


def _mod_row_index(tile, tm):
    return jnp.minimum((tile * tm) // SEQ, BATCH)


def _mod_kernel(cc_ref, w_ref, b_ref, o_ref):
    a = _silu(cc_ref[...]).astype(BF16)
    o_ref[...] = _dot(a, w_ref[...].astype(BF16)) + b_ref[...]


def _modulation(cc, w_mod, b_mod):
    tn = 1024
    n_out = 9 * D_MODEL
    return pl.pallas_call(
        _mod_kernel,
        grid=(DEPTH, n_out // tn),
        in_specs=[
            pl.BlockSpec((MOD_ROWS, D_MODEL), lambda l, j: (0, 0)),
            pl.BlockSpec((None, D_MODEL, tn), lambda l, j: (l, 0, j)),
            pl.BlockSpec((None, 1, tn), lambda l, j: (l, 0, j)),
        ],
        out_specs=pl.BlockSpec((None, MOD_ROWS, tn), lambda l, j: (l, 0, j)),
        out_shape=jax.ShapeDtypeStruct((DEPTH, MOD_ROWS, n_out), F32),
        compiler_params=_cparams(("parallel", "parallel")),
        name="adaln_modulation",
    )(cc, w_mod, b_mod.reshape(DEPTH, 1, n_out))


def _ffn_kernel(x_ref, m_ref, g_ref, wg_ref, wu_ref, wd_ref, o_ref, hm_ref, acc_ref, *, sub, g_pre, g_post):
    j = pl.program_id(1)

    @pl.when(j == 0)
    def _():
        x = x_ref[...]
        y = x * _rms_rows(x) * g_ref[g_pre:g_pre + 1, :]
        hm_ref[...] = (y * (1.0 + m_ref[3 * sub + 1:3 * sub + 2, :]) + m_ref[3 * sub:3 * sub + 1, :]).astype(BF16)

    hm = hm_ref[...]
    gate = _dot(hm, wg_ref[...])
    up = _dot(hm, wu_ref[...])
    contrib = _dot((_silu(gate) * up).astype(BF16), wd_ref[...])

    @pl.when(j == 0)
    def _():
        acc_ref[...] = contrib

    @pl.when(j > 0)
    def _():
        acc_ref[...] += contrib

    @pl.when(j == pl.num_programs(1) - 1)
    def _():
        y = acc_ref[...]
        yn = y * _rms_rows(y) * g_ref[g_post:g_post + 1, :]
        o_ref[...] = x_ref[...] + (FFN_RESIDUAL * m_ref[3 * sub + 2:3 * sub + 3, :]) * yn


def _ffn_sublayer(h, mod, gains, wg, wu, wd, *, sub, rows):
    tm = TM_FFN
    kern = functools.partial(_ffn_kernel, sub=sub, g_pre=2 * sub, g_post=2 * sub + 1)
    return pl.pallas_call(
        kern,
        grid=(rows // tm, D_FF // FF_CHUNK),
        in_specs=[
            pl.BlockSpec((tm, D_MODEL), lambda i, j: (i, 0)),
            pl.BlockSpec((None, 9, D_MODEL), lambda i, j: (_mod_row_index(i, tm), 0, 0)),
            pl.BlockSpec((6, D_MODEL), lambda i, j: (0, 0)),
            pl.BlockSpec((D_MODEL, FF_CHUNK), lambda i, j: (0, j)),
            pl.BlockSpec((D_MODEL, FF_CHUNK), lambda i, j: (0, j)),
            pl.BlockSpec((FF_CHUNK, D_MODEL), lambda i, j: (j, 0)),
        ],
        out_specs=pl.BlockSpec((tm, D_MODEL), lambda i, j: (i, 0)),
        out_shape=jax.ShapeDtypeStruct((rows, D_MODEL), F32),
        scratch_shapes=[pltpu.VMEM((tm, D_MODEL), BF16), pltpu.VMEM((tm, D_MODEL), F32)],
        compiler_params=_cparams(("parallel", "arbitrary")),
        name=f"ffn_sublayer_{sub}",
    )(h, mod, gains, wg, wu, wd)


def _rope(x, tab_ref, quarter):
    w = x.shape[1]
    reps = w // tab_ref.shape[2]

    def tab(k):
        t = tab_ref[k]
        return t if reps == 1 else jnp.concatenate([t] * reps, axis=1)

    return x * tab(0) + pltpu.roll(x, w - quarter, 1) * tab(1) + pltpu.roll(x, quarter, 1) * tab(2)


def _inproj_kernel(x_ref, m_ref, g_ref, w_ref, ta_ref, tb_ref, qkg_ref,
                   qa_ref, ka_ref, va_ref, qb_ref, kb_ref, vb_ref, ucd_ref):
    x = x_ref[...]
    y = x * _rms_rows(x) * g_ref[2:3, :]
    hm = (y * (1.0 + m_ref[4:5, :]) + m_ref[3:4, :]).astype(BF16)

    def proj(lo, hi):
        return _dot(hm, w_ref[:, lo:hi])

    ones = _block_ones(256, HEAD_DIM).astype(BF16)
    inv_d = 1.0 / HEAD_DIM

    q = proj(0, 256)
    q = q * lax.rsqrt(_group_sums(q * q, ones) * inv_d + EPS) * qkg_ref[0:1, :]
    qa_ref[...] = (_rope(q, ta_ref, HEAD_DIM // 4) * (HEAD_DIM ** -0.5)).astype(BF16)
    k = proj(256, 384)
    k = k * lax.rsqrt(_group_sums(k * k, ones[:128, :128]) * inv_d + EPS) * qkg_ref[1:2, :128]
    ka_ref[...] = _rope(k, ta_ref, HEAD_DIM // 4).astype(BF16)
    va_ref[...] = proj(384, 512).astype(BF16)

    qb_ref[...] = (_rope(proj(512, 768), tb_ref, DIFF_QK_DIM // 4) * (DIFF_QK_DIM ** -0.5)).astype(BF16)
    kb_ref[...] = _rope(proj(768, 1024), tb_ref, DIFF_QK_DIM // 4).astype(BF16)
    vb_ref[...] = proj(1024, 1280).astype(BF16)

    ucd_ref[...] = proj(1280, 2560)


def _in_projection(h, mod, gains, w_in, tab_a, tab_b, qk_gain_lanes):
    tm = TM_PROJ
    lat_tiles = T_LAT // tm
    tiles_per_seq = SEQ // tm

    def tab_idx(i):
        return jnp.where(i < lat_tiles, i % tiles_per_seq, tiles_per_seq)

    def row_spec(width):
        return pl.BlockSpec((tm, width), lambda i: (i, 0))

    def out(width, dtype):
        return jax.ShapeDtypeStruct((T_ALL, width), dtype)

    return pl.pallas_call(
        _inproj_kernel,
        grid=(T_ALL // tm,),
        in_specs=[
            row_spec(D_MODEL),
            pl.BlockSpec((None, 9, D_MODEL), lambda i: (_mod_row_index(i, tm), 0, 0)),
            pl.BlockSpec((6, D_MODEL), lambda i: (0, 0)),
            pl.BlockSpec((D_MODEL, D_IN), lambda i: (0, 0)),
            pl.BlockSpec((3, tm, 128), lambda i: (0, tab_idx(i), 0)),
            pl.BlockSpec((3, tm, 128), lambda i: (0, tab_idx(i), 0)),
            pl.BlockSpec((2, 256), lambda i: (0, 0)),
        ],
        out_specs=[row_spec(256), row_spec(128), row_spec(128), row_spec(256), row_spec(256), row_spec(256),
                   row_spec(1280)],
        out_shape=[out(256, BF16), out(128, BF16), out(128, BF16), out(256, BF16), out(256, BF16), out(256, BF16),
                   out(1280, F32)],
        compiler_params=_cparams(("parallel",)),
        name="in_projection",
    )(h, mod, gains, w_in, tab_a, tab_b, qk_gain_lanes)


def _softmax_pv(qs, ks, vs):
    ss = [_dot_nt(qs, k) for k in ks]
    m = ss[0].max(axis=-1, keepdims=True)
    for s in ss[1:]:
        m = jnp.maximum(m, s.max(axis=-1, keepdims=True))
    denom = None
    o = None
    for s, v in zip(ss, vs):
        p = jnp.exp(s - m)
        psum = p.sum(axis=-1, keepdims=True)
        pv = _dot(p.astype(BF16), v)
        denom = psum if denom is None else denom + psum
        o = pv if o is None else o + pv
    return o / denom


def _attn_kernel(*refs, n_parts, lam_init):
    qa_ref = refs[0]
    ka = [r[...] for r in refs[1:1 + n_parts]]
    va = [r[...] for r in refs[1 + n_parts:1 + 2 * n_parts]]
    qb_ref = refs[1 + 2 * n_parts]
    kb = [r[...] for r in refs[2 + 2 * n_parts:2 + 3 * n_parts]]
    vb = [r[...] for r in refs[2 + 3 * n_parts:2 + 4 * n_parts]]
    lam_ref, dg_ref, ya_ref, yb_ref = refs[2 + 4 * n_parts:]
    tq = qa_ref.shape[0]

    lane128 = lax.broadcasted_iota(jnp.int32, (1, 128), 1)
    qa = qa_ref[...]
    zero = jnp.zeros((), BF16)
    outs = []
    for g in range(2):
        sel = (lane128 // HEAD_DIM) == g
        qs = jnp.concatenate([jnp.where(sel, qa[:, 0:128], zero), jnp.where(sel, qa[:, 128:256], zero)], axis=0)
        outs.append(_softmax_pv(qs, ka, va))
    first = lane128 < HEAD_DIM
    y0 = jnp.where(first, outs[0][:tq], outs[1][:tq])
    y1 = jnp.where(first, outs[0][tq:], outs[1][tq:])
    ya_ref[...] = jnp.concatenate([y0, y1], axis=1).astype(BF16)

    lp = lam_ref[...]
    lam = (jnp.exp(jnp.sum(lp[0:1] * lp[1:2], axis=-1, keepdims=True))
           - jnp.exp(jnp.sum(lp[2:3] * lp[3:4], axis=-1, keepdims=True)) + lam_init)
    lane256 = lax.broadcasted_iota(jnp.int32, (1, 256), 1)
    qb = qb_ref[...]
    y = jnp.zeros((tq, 256), F32)
    for h in range(4):
        half = lane256 // DIFF_QK_DIM
        qs = jnp.concatenate([jnp.where(half == 2 * h, qb, zero), jnp.where(half == 2 * h + 1, qb, zero)], axis=0)
        o = _softmax_pv(qs, kb, vb)
        y = jnp.where((lane256 // HEAD_DIM) == h, o[:tq] - lam * o[tq:], y)
    ones = _block_ones(256, HEAD_DIM).astype(BF16)
    yn = y * lax.rsqrt(_group_sums(y * y, ones) * (1.0 / HEAD_DIM) + EPS) * dg_ref[...]
    yb_ref[...] = (yn * (1.0 - lam_init)).astype(BF16)


def _attention(qa, ka, va, qb, kb, vb, diff_lambda, diff_gain_lanes, *, lam_init, latent):
    ctx_blk0 = T_LAT // CTX_LEN
    if latent:
        grid = (BATCH, SEQ // TQ)
        q_map = lambda b, i: (b * (SEQ // TQ) + i, 0)
        kv_specs = lambda w: [pl.BlockSpec((CTX_LEN, w), lambda b, i: (ctx_blk0 + b, 0)),
                              pl.BlockSpec((SEQ, w), lambda b, i: (b, 0))]
        rows = T_LAT
        n_parts = 2
        const = lambda b, i: (0, 0)
        o_map = q_map
    else:
        grid = (BATCH,)
        q_map = lambda b: (ctx_blk0 + b, 0)
        kv_specs = lambda w: [pl.BlockSpec((CTX_LEN, w), lambda b: (ctx_blk0 + b, 0))]
        rows = T_CTX
        n_parts = 1
        const = lambda b: (0, 0)
        o_map = lambda b: (b, 0)
    in_specs = ([pl.BlockSpec((TQ, 256), q_map)] + kv_specs(128) + kv_specs(128)
                + [pl.BlockSpec((TQ, 256), q_map)] + kv_specs(256) + kv_specs(256)
                + [pl.BlockSpec((4, DIFF_QK_DIM), const), pl.BlockSpec((1, 256), const)])
    args = [qa] + [ka] * n_parts + [va] * n_parts + [qb] + [kb] * n_parts + [vb] * n_parts
    return pl.pallas_call(
        functools.partial(_attn_kernel, n_parts=n_parts, lam_init=lam_init),
        grid=grid,
        in_specs=in_specs,
        out_specs=[pl.BlockSpec((TQ, 256), o_map), pl.BlockSpec((TQ, 256), o_map)],
        out_shape=[jax.ShapeDtypeStruct((rows, 256), BF16), jax.ShapeDtypeStruct((rows, 256), BF16)],
        compiler_params=_cparams(("parallel",) * len(grid)),
        name="attention_latent" if latent else "attention_context",
    )(*args, diff_lambda, diff_gain_lanes)


def _pool(u, pw, ps):
    n_seq = u.shape[0]
    pad = 16
    n = n_seq + 2 * pad
    z = jnp.zeros((pad, 256), F32)
    p1 = jnp.concatenate([z, u, z], axis=0)
    s2 = p1 + pltpu.roll(p1, n - 1, 0)
    s4 = s2 + pltpu.roll(s2, n - 2, 0)
    s8 = s4 + pltpu.roll(s4, n - 4, 0)
    s16 = s8 + pltpu.roll(s8, n - 8, 0)
    w2 = pltpu.roll(s2, 1, 0)[pad:pad + n_seq]
    w4 = pltpu.roll(s4, 2, 0)[pad:pad + n_seq]
    w8 = pltpu.roll(s8, 4, 0)[pad:pad + n_seq]
    w16 = pltpu.roll(s16, 8, 0)[pad:pad + n_seq]
    grp = lax.broadcasted_iota(jnp.int32, (1, 256), 1) // 64
    wsum = jnp.where(grp == 0, w2, jnp.where(grp == 1, w4, jnp.where(grp == 2, w8, w16)))
    back = jnp.where(grp == 0, 1, jnp.where(grp == 1, 2, jnp.where(grp == 2, 4, 8)))
    t = lax.broadcasted_iota(jnp.int32, (n_seq, 256), 0)
    count = jnp.minimum(t + (back - 1), n_seq - 1) - jnp.maximum(t - back, 0) + 1
    pooled = wsum / count.astype(F32) - u
    return _dot(pooled.astype(BF16), pw) * ps


def _log_sigmoid(x):
    return jnp.minimum(x, 0.0) - jnp.log(1.0 + jnp.exp(-jnp.abs(x)))


def _mix_kernel(*refs, with_ctx):
    if with_ctx:
        (ul_ref, uc_ref, dl_ref, dh_ref, rg_ref, pw_ref, ps_ref,
         ycl_ref, ydl_ref, ycc_ref, ydc_ref, tst_ref) = refs
    else:
        (ul_ref, uc_ref, dl_ref, dh_ref, rg_ref, pw_ref, ps_ref, ycl_ref, ydl_ref, tst_ref) = refs
    c = RET_C
    n_chunks = SEQ // c
    pw = pw_ref[...]
    ps = ps_ref[...]

    ycl_ref[...] = _pool(ul_ref[:, 0:256], pw, ps).astype(BF16)
    if with_ctx:
        ycc_ref[...] = _pool(uc_ref[:, 0:256], pw, ps).astype(BF16)

    lg = _log_sigmoid(dl_ref[...])
    lgh = _log_sigmoid(dh_ref[...])
    lgf, lgb = lg[0:1, :], lg[1:2, :]
    pos = lax.broadcasted_iota(jnp.int32, (c, 1), 0).astype(F32)
    wkf = jnp.exp((c - 1.0 - pos) * lgf)
    wkb = jnp.exp(pos * lgb)
    wqf = jnp.exp((pos + 1.0) * lgf)
    wqb = jnp.exp((c - pos) * lgb)
    dcf = jnp.exp(c * lgf)
    dcb = jnp.exp(c * lgb)
    same_head = _block_ones(256, HEAD_DIM)
    ones = same_head.astype(BF16)
    lane = lax.broadcasted_iota(jnp.int32, (1, 256), 1) // HEAD_DIM
    diff = (lax.broadcasted_iota(jnp.int32, (c, c), 0) - lax.broadcasted_iota(jnp.int32, (c, c), 1)).astype(F32)
    decay = []
    for h in range(4):
        df = jnp.exp(jnp.maximum(diff, 0.0) * lgh[h:h + 1, 0:1])
        db = jnp.exp(jnp.maximum(-diff, 0.0) * lgh[4 + h:5 + h, 0:1])
        decay.append(jnp.where(diff > 0, df, jnp.where(diff < 0, db, 2.0)))
    rg = rg_ref[...]
    k_scale = HEAD_DIM ** -0.5

    def load(ref, r0):
        rows = pl.ds(r0, c)
        return ref[rows, 256:512], ref[rows, 512:768] * k_scale, ref[rows, 768:1024], ref[rows, 1024:1280]

    def increment(k, v, wk):
        return jnp.where(same_head, _dot_tn((k * wk).astype(BF16), v.astype(BF16)), 0.0)

    def chunk_out(q, k, v, gate, s_fwd, s_bwd):
        o = _dot((q * wqf).astype(BF16), s_fwd.astype(BF16)) + _dot((q * wqb).astype(BF16), s_bwd.astype(BF16))
        kb = k.astype(BF16)
        for h in range(4):
            sel = lane == h
            s = _dot_nt(jnp.where(sel, q, 0.0).astype(BF16), kb)
            o = o + _dot((s * decay[h]).astype(BF16), jnp.where(sel, v, 0.0).astype(BF16))
        on = o * lax.rsqrt(_group_sums(o * o, ones) * (1.0 / HEAD_DIM) + EPS) * rg
        return (on * _silu(gate)).astype(BF16)

    _, kc, vc, _ = load(uc_ref, 0)
    s_fwd = increment(kc, vc, wkf)
    s_bwd = increment(kc, vc, wkb)

    for j in range(n_chunks - 1, -1, -1):
        tst_ref[j] = s_bwd
        if j > 0:
            _, k, v, _ = load(ul_ref, j * c)
            s_bwd = s_bwd * dcb + increment(k, v, wkb)

    for j in range(n_chunks):
        q, k, v, gate = load(ul_ref, j * c)
        ydl_ref[pl.ds(j * c, c), :] = chunk_out(q, k, v, gate, s_fwd, tst_ref[j])
        if j < n_chunks - 1:
            s_fwd = s_fwd * dcf + increment(k, v, wkf)

    if with_ctx:
        qc, kc, vc, gc = load(uc_ref, 0)
        zeros = jnp.zeros((256, 256), F32)
        ydc_ref[...] = chunk_out(qc, kc, vc, gc, zeros, zeros)


def _mixer(ucd, decay_lanes, decay_heads, ret_gain_lanes, pool_w_bd, pool_scale, *, with_ctx):
    ctx_blk0 = T_LAT // CTX_LEN
    const = lambda b: (0, 0)
    out_specs = [pl.BlockSpec((SEQ, 256), lambda b: (b, 0))] * 2
    out_shape = [jax.ShapeDtypeStruct((T_LAT, 256), BF16)] * 2
    if with_ctx:
        out_specs = out_specs + [pl.BlockSpec((CTX_LEN, 256), lambda b: (b, 0))] * 2
        out_shape = out_shape + [jax.ShapeDtypeStruct((T_CTX, 256), BF16)] * 2
    return pl.pallas_call(
        functools.partial(_mix_kernel, with_ctx=with_ctx),
        grid=(BATCH,),
        in_specs=[
            pl.BlockSpec((SEQ, 1280), lambda b: (b, 0)),
            pl.BlockSpec((CTX_LEN, 1280), lambda b: (ctx_blk0 + b, 0)),
            pl.BlockSpec((2, 256), const),
            pl.BlockSpec((8, 128), const),
            pl.BlockSpec((1, 256), const),
            pl.BlockSpec((256, 256), const),
            pl.BlockSpec((1, 256), const),
        ],
        out_specs=out_specs,
        out_shape=out_shape,
        scratch_shapes=[pltpu.VMEM((SEQ // RET_C, 256, 256), F32)],
        compiler_params=_cparams(("parallel",)),
        name="pool_retention_mixer",
    )(ucd, ucd, decay_lanes, decay_heads, ret_gain_lanes, pool_w_bd, pool_scale)


def _outproj_kernel(x_ref, m_ref, g_ref, ya_ref, yb_ref, yc_ref, yd_ref, w_ref, o_ref):
    y = (_dot(ya_ref[...], w_ref[0:256, :]) + _dot(yb_ref[...], w_ref[256:512, :])
         + _dot(yc_ref[...], w_ref[512:768, :]) + _dot(yd_ref[...], w_ref[768:1024, :]))
    yn = y * _rms_rows(y) * g_ref[3:4, :]
    o_ref[...] = x_ref[...] + m_ref[5:6, :] * yn


def _out_projection(h, mod, gains, ya, yb, yc, yd, w_out, *, rows):
    tm = TM_PROJ
    part = pl.BlockSpec((tm, 256), lambda i: (i, 0))
    return pl.pallas_call(
        _outproj_kernel,
        grid=(rows // tm,),
        in_specs=[
            pl.BlockSpec((tm, D_MODEL), lambda i: (i, 0)),
            pl.BlockSpec((None, 9, D_MODEL), lambda i: (_mod_row_index(i, tm), 0, 0)),
            pl.BlockSpec((6, D_MODEL), lambda i: (0, 0)),
            part, part, part, part,
            pl.BlockSpec((D_MODEL, D_MODEL), lambda i: (0, 0)),
        ],
        out_specs=pl.BlockSpec((tm, D_MODEL), lambda i: (i, 0)),
        out_shape=jax.ShapeDtypeStruct((rows, D_MODEL), F32),
        compiler_params=_cparams(("parallel",)),
        name="out_projection",
    )(h, mod, gains, ya, yb, yc, yd, w_out)


def _rope_tables(dim, tm):
    q = dim // 4
    n_rows = SEQ // GRID_W
    rows = jnp.repeat(jnp.arange(n_rows), GRID_W).astype(F32)
    cols = jnp.tile(jnp.arange(GRID_W), n_rows).astype(F32)
    inv = ROPE_BASE ** (-jnp.arange(q, dtype=F32) / q)
    ar, ac = rows[:, None] * inv, cols[:, None] * inv
    cos = jnp.concatenate([jnp.cos(ar)] * 2 + [jnp.cos(ac)] * 2, axis=-1)
    sin = jnp.concatenate([jnp.sin(ar)] * 2 + [jnp.sin(ac)] * 2, axis=-1)
    first_half = jnp.tile(jnp.repeat(jnp.array([True, False]), q), 2)
    tabs = jnp.stack([cos, jnp.where(first_half, -sin, 0.0), jnp.where(first_half, 0.0, sin)])
    tabs = jnp.tile(tabs, (1, 1, 128 // dim))
    ident = jnp.stack([jnp.ones((tm, 128), F32), jnp.zeros((tm, 128), F32), jnp.zeros((tm, 128), F32)])
    return jnp.concatenate([tabs, ident], axis=1)


def _block_diag(blocks):
    n, d, _ = blocks.shape
    eye = jnp.eye(n, dtype=blocks.dtype)
    return jnp.einsum('gcd,gh->gchd', blocks, eye).reshape(n * d, n * d)


_GQA_HEAD_ORDER = (0, 2, 1, 3)


def kernel(x, c, ctx, c_ctx, w_mod, b_mod, norm_gain, ffn_w_gate, ffn_w_up, ffn_w_down, w_in, w_out,
           attn_qk_gain, diff_lambda, diff_out_gain, pool_w, pool_scale, ret_decay_logit, ret_out_gain):
    h = jnp.concatenate([x.reshape(T_LAT, D_MODEL), ctx.reshape(T_CTX, D_MODEL)], axis=0)
    cc = jnp.concatenate([c, c_ctx[None, :], jnp.zeros((MOD_ROWS - BATCH - 1, D_MODEL), F32)], axis=0)
    mod_all = _modulation(cc, w_mod, b_mod).reshape(DEPTH, MOD_ROWS, 9, D_MODEL)
    tab_a = _rope_tables(HEAD_DIM, TM_PROJ)
    tab_b = _rope_tables(DIFF_QK_DIM, TM_PROJ)
    head_perm = jnp.concatenate([jnp.arange(HEAD_DIM) + HEAD_DIM * hd for hd in _GQA_HEAD_ORDER])

    for i in range(DEPTH):
        last = i == DEPTH - 1
        mod = mod_all[i]
        gains = norm_gain[i]
        w_in_i = jnp.concatenate([w_in[i][:, head_perm], w_in[i][:, 256:]], axis=1).astype(BF16)
        w_out_i = jnp.concatenate([w_out[i][head_perm, :], w_out[i][256:, :]], axis=0).astype(BF16)
        qk_gain_lanes = jnp.tile(attn_qk_gain[i], (1, 256 // HEAD_DIM))
        diff_gain_lanes = jnp.tile(diff_out_gain[i], 256 // HEAD_DIM)[None, :]
        ret_gain_lanes = jnp.tile(ret_out_gain[i], 256 // HEAD_DIM)[None, :]
        decay_lanes = jnp.repeat(ret_decay_logit[i], HEAD_DIM, axis=1)
        decay_heads = jnp.broadcast_to(ret_decay_logit[i].reshape(8, 1), (8, 128))
        pool_w_bd = _block_diag(pool_w[i]).astype(BF16)
        lam_init = 0.8 - 0.6 * math.exp(-0.3 * i)

        def ffn(hh, sub, rows):
            k = sub // 2
            return _ffn_sublayer(hh, mod, gains, ffn_w_gate[i, k].astype(BF16), ffn_w_up[i, k].astype(BF16),
                                 ffn_w_down[i, k].astype(BF16), sub=sub, rows=rows)

        h = ffn(h, 0, T_ALL)
        qa, ka, va, qb, kb, vb, ucd = _in_projection(h, mod, gains, w_in_i, tab_a, tab_b, qk_gain_lanes)
        ya, yb = _attention(qa, ka, va, qb, kb, vb, diff_lambda[i], diff_gain_lanes, lam_init=lam_init, latent=True)
        mixed = _mixer(ucd, decay_lanes, decay_heads, ret_gain_lanes, pool_w_bd, pool_scale[i][None, :],
                       with_ctx=not last)
        yc, yd = mixed[0], mixed[1]
        rows = T_LAT
        if not last:
            ya_c, yb_c = _attention(qa, ka, va, qb, kb, vb, diff_lambda[i], diff_gain_lanes, lam_init=lam_init,
                                    latent=False)
            ya, yb = jnp.concatenate([ya, ya_c], axis=0), jnp.concatenate([yb, yb_c], axis=0)
            yc, yd = jnp.concatenate([yc, mixed[2]], axis=0), jnp.concatenate([yd, mixed[3]], axis=0)
            rows = T_ALL
        h = _out_projection(h, mod, gains, ya, yb, yc, yd, w_out_i, rows=rows)
        h = ffn(h, 2, rows)
    return h.reshape(BATCH, SEQ, D_MODEL)
```

```python
import functools
import math

import jax
import jax.numpy as jnp
from jax import lax
from jax.experimental import pallas as pl
from jax.experimental.pallas import tpu as pltpu

F32 = jnp.float32
BF16 = jnp.bfloat16

D_MODEL = 1024
BATCH = 8
SEQ = 2048
DEPTH = 2
CTX_LEN = 256
GRID_W = 64
HEAD_DIM = 64
DIFF_QK_DIM = 32
GROUP_WIDTH = 256
D_FF = 2816
D_IN = 2560
FFN_RESIDUAL = 0.5
ROPE_BASE = 10000.0
EPS = 1e-6

T_LAT = BATCH * SEQ
T_CTX = BATCH * CTX_LEN
T_ALL = T_LAT + T_CTX
MOD_ROWS = 16

TM_FFN = 512
FFN_SPLIT = 2
TM_PROJ = 512
LOG2E = 1.4426950408889634
TQ = 256
RET_C = 256
VMEM_LIMIT = 56 * 1024 * 1024


def _cparams(sem):
    return pltpu.CompilerParams(dimension_semantics=sem, vmem_limit_bytes=VMEM_LIMIT)


def _silu(x):
    return x * jax.nn.sigmoid(x)


def _dot(a, b):
    return jnp.dot(a, b, preferred_element_type=F32)


def _dot_nt(a, b):
    return lax.dot_general(a, b, (((1,), (1,)), ((), ())), preferred_element_type=F32)


def _dot_tn(a, b):
    return lax.dot_general(a, b, (((0,), (0,)), ((), ())), preferred_element_type=F32)


def _block_ones(n, blk):
    r = lax.broadcasted_iota(jnp.int32, (n, n), 0) // blk
    c = lax.broadcasted_iota(jnp.int32, (n, n), 1) // blk
    return r == c


def _group_sums(sq, ones_bf16):
    hi = sq.astype(BF16)
    lo = (sq - hi.astype(F32)).astype(BF16)
    return _dot(hi, ones_bf16) + _dot(lo, ones_bf16)


def _rms_rows(x):
    return lax.rsqrt(jnp.mean(x * x, axis=-1, keepdims=True) + EPS)


def _mod_row_index(tile, tm):
    return jnp.minimum((tile * tm) // SEQ, BATCH)


def _mod_kernel(cc_ref, w_ref, b_ref, o_ref):
    a = _silu(cc_ref[...]).astype(BF16)
    o_ref[...] = _dot(a, w_ref[...].astype(BF16)) + b_ref[...]


def _modulation(cc, w_mod, b_mod):
    tn = 1024
    n_out = 9 * D_MODEL
    return pl.pallas_call(
        _mod_kernel,
        grid=(DEPTH, n_out // tn),
        in_specs=[
            pl.BlockSpec((MOD_ROWS, D_MODEL), lambda l, j: (0, 0)),
            pl.BlockSpec((None, D_MODEL, tn), lambda l, j: (l, 0, j)),
            pl.BlockSpec((None, 1, tn), lambda l, j: (l, 0, j)),
        ],
        out_specs=pl.BlockSpec((None, MOD_ROWS, tn), lambda l, j: (l, 0, j)),
        out_shape=jax.ShapeDtypeStruct((DEPTH, MOD_ROWS, n_out), F32),
        compiler_params=_cparams(("parallel", "parallel")),
        name="adaln_modulation",
    )(cc, w_mod, b_mod.reshape(DEPTH, 1, n_out))


def _token_tile(refs, rows, lat_tiles):
    if len(refs) == 1:
        return refs[0][rows, :]
    return jnp.where(pl.program_id(0) < lat_tiles, refs[0][rows, :], refs[1][rows, :])


def _ffn_kernel(*refs, n_x, sub, g_pre, g_post, lat_tiles):
    x_refs = refs[:n_x]
    m_ref, g_ref, wg_ref, wu_ref, wd_ref, o_ref = refs[n_x:]
    pre = g_ref[g_pre:g_pre + 1, :] * (1.0 + m_ref[3 * sub + 1:3 * sub + 2, :])
    shift = m_ref[3 * sub:3 * sub + 1, :]
    post = (FFN_RESIDUAL * m_ref[3 * sub + 2:3 * sub + 3, :]) * g_ref[g_post:g_post + 1, :]
    rs = o_ref.shape[0] // FFN_SPLIT
    for s in range(FFN_SPLIT):
        rows = pl.ds(s * rs, rs)
        x = _token_tile(x_refs, rows, lat_tiles)
        hm = (x * _rms_rows(x) * pre + shift).astype(BF16)
        gate = _dot(hm, wg_ref[...])
        up = _dot(hm, wu_ref[...])
        y = _dot((_silu(gate) * up).astype(BF16), wd_ref[...])
        o_ref[rows, :] = x + (y * _rms_rows(y)) * post


def _stream_specs(xs, tm, width):
    if len(xs) == 1:
        return [pl.BlockSpec((tm, width), lambda i: (i, 0))]
    lat_tiles = xs[0].shape[0] // tm
    return [pl.BlockSpec((tm, width), lambda i: (jnp.minimum(i, lat_tiles - 1), 0)),
            pl.BlockSpec((tm, width), lambda i: (jnp.maximum(i - lat_tiles, 0), 0))]


def _ffn_sublayer(xs, mod, gains, wg, wu, wd, *, layer, sub, rows):
    tm = TM_FFN
    kern = functools.partial(_ffn_kernel, n_x=len(xs), sub=sub, g_pre=2 * sub, g_post=2 * sub + 1,
                             lat_tiles=T_LAT // tm)
    k = sub // 2
    resident = dict(pipeline_mode=pl.Buffered(1))
    return pl.pallas_call(
        kern,
        grid=(rows // tm,),
        in_specs=_stream_specs(xs, tm, D_MODEL) + [
            pl.BlockSpec((None, 9, D_MODEL), lambda i: (_mod_row_index(i, tm), 0, 0)),
            pl.BlockSpec((6, D_MODEL), lambda i: (0, 0)),
            pl.BlockSpec((None, None, D_MODEL, D_FF), lambda i: (layer, k, 0, 0), **resident),
            pl.BlockSpec((None, None, D_MODEL, D_FF), lambda i: (layer, k, 0, 0), **resident),
            pl.BlockSpec((None, None, D_FF, D_MODEL), lambda i: (layer, k, 0, 0), **resident),
        ],
        out_specs=pl.BlockSpec((tm, D_MODEL), lambda i: (i, 0)),
        out_shape=jax.ShapeDtypeStruct((rows, D_MODEL), F32),
        compiler_params=_cparams(("parallel",)),
        name=f"ffn_sublayer_{sub}",
    )(*xs, mod, gains, wg, wu, wd)


def _rope(x, tab_ref, quarter):
    w = x.shape[1]
    reps = w // tab_ref.shape[2]

    def tab(k):
        t = tab_ref[k]
        return t if reps == 1 else jnp.concatenate([t] * reps, axis=1)

    return x * tab(0) + pltpu.roll(x, w - quarter, 1) * tab(1) + pltpu.roll(x, quarter, 1) * tab(2)


def _inproj_kernel(x_ref, m_ref, g_ref, w_ref, ta_ref, tb_ref, qkg_ref,
                   qa_ref, ka_ref, va_ref, qb_ref, kb_ref, vb_ref, ucd_ref):
    x = x_ref[...]
    y = x * _rms_rows(x) * g_ref[2:3, :]
    hm = (y * (1.0 + m_ref[4:5, :]) + m_ref[3:4, :]).astype(BF16)

    def proj(lo, hi):
        return _dot(hm, w_ref[:, lo:hi])

    ones = _block_ones(256, HEAD_DIM).astype(BF16)
    inv_d = 1.0 / HEAD_DIM

    q = proj(0, 256)
    q = q * lax.rsqrt(_group_sums(q * q, ones) * inv_d + EPS) * qkg_ref[0:1, :]
    qa_ref[...] = (_rope(q, ta_ref, HEAD_DIM // 4) * (HEAD_DIM ** -0.5 * LOG2E)).astype(BF16)
    k = proj(256, 384)
    k = k * lax.rsqrt(_group_sums(k * k, ones[:128, :128]) * inv_d + EPS) * qkg_ref[1:2, :128]
    ka_ref[...] = _rope(k, ta_ref, HEAD_DIM // 4).astype(BF16)
    v = proj(384, 512)
    lane128 = lax.broadcasted_iota(jnp.int32, (1, 128), 1)
    for g in range(2):
        va_ref[:, 128 * g:128 * g + 128] = jnp.where(lane128 == HEAD_DIM * (1 - g), 1.0, v).astype(BF16)

    qb_ref[...] = (_rope(proj(512, 768), tb_ref, DIFF_QK_DIM // 4) * (DIFF_QK_DIM ** -0.5 * LOG2E)).astype(BF16)
    kb_ref[...] = _rope(proj(768, 1024), tb_ref, DIFF_QK_DIM // 4).astype(BF16)
    v = proj(1024, 1280)
    lane256 = lax.broadcasted_iota(jnp.int32, (1, 256), 1)
    for hd in range(4):
        ones_lane = HEAD_DIM * ((hd + 1) % 4)
        vb_ref[:, 256 * hd:256 * hd + 256] = jnp.where(lane256 == ones_lane, 1.0, v).astype(BF16)

    ucd_ref[...] = proj(1280, 2560)


def _in_projection(h, mod, gains, w_in, tab_a, tab_b, qk_gain_lanes, *, layer):
    tm = TM_PROJ
    lat_tiles = T_LAT // tm
    tiles_per_seq = SEQ // tm

    def tab_idx(i):
        return jnp.where(i < lat_tiles, i % tiles_per_seq, tiles_per_seq)

    def row_spec(width):
        return pl.BlockSpec((tm, width), lambda i: (i, 0))

    def out(width, dtype):
        return jax.ShapeDtypeStruct((T_ALL, width), dtype)

    return pl.pallas_call(
        _inproj_kernel,
        grid=(T_ALL // tm,),
        in_specs=[
            row_spec(D_MODEL),
            pl.BlockSpec((None, 9, D_MODEL), lambda i: (_mod_row_index(i, tm), 0, 0)),
            pl.BlockSpec((6, D_MODEL), lambda i: (0, 0)),
            pl.BlockSpec((None, D_MODEL, D_IN), lambda i: (layer, 0, 0), pipeline_mode=pl.Buffered(1)),
            pl.BlockSpec((3, tm, 128), lambda i: (0, tab_idx(i), 0)),
            pl.BlockSpec((3, tm, 128), lambda i: (0, tab_idx(i), 0)),
            pl.BlockSpec((2, 256), lambda i: (0, 0)),
        ],
        out_specs=[row_spec(256), row_spec(128), row_spec(256), row_spec(256), row_spec(256), row_spec(1024),
                   row_spec(1280)],
        out_shape=[out(256, BF16), out(128, BF16), out(256, BF16), out(256, BF16), out(256, BF16), out(1024, BF16),
                   out(1280, F32)],
        compiler_params=_cparams(("parallel",)),
        name="in_projection",
    )(h, mod, gains, w_in, tab_a, tab_b, qk_gain_lanes)


def _softmax_pv(qs, k_refs, v_refs, lanes, ones_lane):
    ss = [_dot_nt(qs, k[...]) for k in k_refs]
    m = ss[0].max(axis=-1, keepdims=True)
    for s in ss[1:]:
        m = jnp.maximum(m, s.max(axis=-1, keepdims=True))
    o = None
    for s, v in zip(ss, v_refs):
        pv = _dot(jnp.exp2(s - m).astype(BF16), v[:, lanes])
        o = pv if o is None else o + pv
    return o / o[:, ones_lane:ones_lane + 1]


def _attn_kernel(*refs, n_parts, lam_init):
    qa_ref = refs[0]
    ka = refs[1:1 + n_parts]
    va = refs[1 + n_parts:1 + 2 * n_parts]
    qb_ref = refs[1 + 2 * n_parts]
    kb = refs[2 + 2 * n_parts:2 + 3 * n_parts]
    vb = refs[2 + 3 * n_parts:2 + 4 * n_parts]
    lam_ref, dg_ref, ya_ref, yb_ref = refs[2 + 4 * n_parts:]
    tq = qa_ref.shape[0]

    lane128 = lax.broadcasted_iota(jnp.int32, (1, 128), 1)
    qa = qa_ref[...]
    zero = jnp.zeros((), BF16)
    outs = []
    for g in range(2):
        sel = (lane128 // HEAD_DIM) == g
        qs = jnp.concatenate([jnp.where(sel, qa[:, 0:128], zero), jnp.where(sel, qa[:, 128:256], zero)], axis=0)
        outs.append(_softmax_pv(qs, ka, va, slice(128 * g, 128 * g + 128), HEAD_DIM * (1 - g)))
    first = lane128 < HEAD_DIM
    y0 = jnp.where(first, outs[0][:tq], outs[1][:tq])
    y1 = jnp.where(first, outs[0][tq:], outs[1][tq:])
    ya_ref[...] = jnp.concatenate([y0, y1], axis=1).astype(BF16)

    lp = lam_ref[...]
    lam = (jnp.exp(jnp.sum(lp[0:1] * lp[1:2], axis=-1, keepdims=True))
           - jnp.exp(jnp.sum(lp[2:3] * lp[3:4], axis=-1, keepdims=True)) + lam_init)
    lane256 = lax.broadcasted_iota(jnp.int32, (1, 256), 1)
    qb = qb_ref[...]
    y = jnp.zeros((tq, 256), F32)
    for h in range(4):
        half = lane256 // DIFF_QK_DIM
        qs = jnp.concatenate([jnp.where(half == 2 * h, qb, zero), jnp.where(half == 2 * h + 1, qb, zero)], axis=0)
        o = _softmax_pv(qs, kb, vb, slice(256 * h, 256 * h + 256), HEAD_DIM * ((h + 1) % 4))
        y = jnp.where((lane256 // HEAD_DIM) == h, o[:tq] - lam * o[tq:], y)
    ones = _block_ones(256, HEAD_DIM).astype(BF16)
    yn = y * lax.rsqrt(_group_sums(y * y, ones) * (1.0 / HEAD_DIM) + EPS) * dg_ref[...]
    yb_ref[...] = (yn * (1.0 - lam_init)).astype(BF16)


def _attention(qa, ka, va, qb, kb, vb, diff_lambda, diff_gain_lanes, *, lam_init, latent):
    ctx_blk0 = T_LAT // CTX_LEN
    if latent:
        grid = (BATCH, SEQ // TQ)
        q_map = lambda b, i: (b * (SEQ // TQ) + i, 0)
        kv_specs = lambda w: [pl.BlockSpec((CTX_LEN, w), lambda b, i: (ctx_blk0 + b, 0)),
                              pl.BlockSpec((SEQ, w), lambda b, i: (b, 0))]
        rows = T_LAT
        n_parts = 2
        const = lambda b, i: (0, 0)
        o_map = q_map
    else:
        grid = (BATCH,)
        q_map = lambda b: (ctx_blk0 + b, 0)
        kv_specs = lambda w: [pl.BlockSpec((CTX_LEN, w), lambda b: (ctx_blk0 + b, 0))]
        rows = T_CTX
        n_parts = 1
        const = lambda b: (0, 0)
        o_map = lambda b: (b, 0)
    in_specs = ([pl.BlockSpec((TQ, 256), q_map)] + kv_specs(128) + kv_specs(256)
                + [pl.BlockSpec((TQ, 256), q_map)] + kv_specs(256) + kv_specs(1024)
                + [pl.BlockSpec((4, DIFF_QK_DIM), const), pl.BlockSpec((1, 256), const)])
    args = [qa] + [ka] * n_parts + [va] * n_parts + [qb] + [kb] * n_parts + [vb] * n_parts
    return pl.pallas_call(
        functools.partial(_attn_kernel, n_parts=n_parts, lam_init=lam_init),
        grid=grid,
        in_specs=in_specs,
        out_specs=[pl.BlockSpec((TQ, 256), o_map), pl.BlockSpec((TQ, 256), o_map)],
        out_shape=[jax.ShapeDtypeStruct((rows, 256), BF16), jax.ShapeDtypeStruct((rows, 256), BF16)],
        compiler_params=_cparams(("parallel",) * len(grid)),
        name="attention_latent" if latent else "attention_context",
    )(*args, diff_lambda, diff_gain_lanes)


def _pool(u, pw, ps):
    n_seq = u.shape[0]
    pad = 16
    n = n_seq + 2 * pad
    z = jnp.zeros((pad, 256), F32)
    p1 = jnp.concatenate([z, u, z], axis=0)
    s2 = p1 + pltpu.roll(p1, n - 1, 0)
    s4 = s2 + pltpu.roll(s2, n - 2, 0)
    s8 = s4 + pltpu.roll(s4, n - 4, 0)
    s16 = s8 + pltpu.roll(s8, n - 8, 0)
    w2 = pltpu.roll(s2, 1, 0)[pad:pad + n_seq]
    w4 = pltpu.roll(s4, 2, 0)[pad:pad + n_seq]
    w8 = pltpu.roll(s8, 4, 0)[pad:pad + n_seq]
    w16 = pltpu.roll(s16, 8, 0)[pad:pad + n_seq]
    grp = lax.broadcasted_iota(jnp.int32, (1, 256), 1) // 64
    wsum = jnp.where(grp == 0, w2, jnp.where(grp == 1, w4, jnp.where(grp == 2, w8, w16)))
    back = jnp.where(grp == 0, 1, jnp.where(grp == 1, 2, jnp.where(grp == 2, 4, 8)))
    t = lax.broadcasted_iota(jnp.int32, (n_seq, 256), 0)
    count = jnp.minimum(t + (back - 1), n_seq - 1) - jnp.maximum(t - back, 0) + 1
    pooled = wsum / count.astype(F32) - u
    return _dot(pooled.astype(BF16), pw) * ps


def _log_sigmoid(x):
    return jnp.minimum(x, 0.0) - jnp.log(1.0 + jnp.exp(-jnp.abs(x)))


def _mix_kernel(*refs, with_ctx):
    if with_ctx:
        (ul_ref, uc_ref, dl_ref, dh_ref, rg_ref, pw_ref, ps_ref,
         ycl_ref, ydl_ref, ycc_ref, ydc_ref, tst_ref) = refs
    else:
        (ul_ref, uc_ref, dl_ref, dh_ref, rg_ref, pw_ref, ps_ref, ycl_ref, ydl_ref, tst_ref) = refs
    c = RET_C
    n_chunks = SEQ // c
    pw = pw_ref[...]
    ps = ps_ref[...]

    ycl_ref[...] = _pool(ul_ref[:, 0:256], pw, ps).astype(BF16)
    if with_ctx:
        ycc_ref[...] = _pool(uc_ref[:, 0:256], pw, ps).astype(BF16)

    lg = _log_sigmoid(dl_ref[...])
    lgh = _log_sigmoid(dh_ref[...])
    lgf, lgb = lg[0:1, :], lg[1:2, :]
    pos = lax.broadcasted_iota(jnp.int32, (c, 1), 0).astype(F32)
    wkf = jnp.exp((c - 1.0 - pos) * lgf)
    wkb = jnp.exp(pos * lgb)
    wqf = jnp.exp((pos + 1.0) * lgf)
    wqb = jnp.exp((c - pos) * lgb)
    dcf = jnp.exp(c * lgf)
    dcb = jnp.exp(c * lgb)
    same_head = _block_ones(256, HEAD_DIM)
    ones = same_head.astype(BF16)
    lane = lax.broadcasted_iota(jnp.int32, (1, 256), 1) // HEAD_DIM
    diff = (lax.broadcasted_iota(jnp.int32, (c, c), 0) - lax.broadcasted_iota(jnp.int32, (c, c), 1)).astype(F32)
    decay = []
    for h in range(4):
        df = jnp.exp(jnp.maximum(diff, 0.0) * lgh[h:h + 1, 0:1])
        db = jnp.exp(jnp.maximum(-diff, 0.0) * lgh[4 + h:5 + h, 0:1])
        decay.append(jnp.where(diff > 0, df, jnp.where(diff < 0, db, 2.0)))
    rg = rg_ref[...]
    k_scale = HEAD_DIM ** -0.5

    def load(ref, r0):
        rows = pl.ds(r0, c)
        return ref[rows, 256:512], ref[rows, 512:768] * k_scale, ref[rows, 768:1024], ref[rows, 1024:1280]

    def increment(k, v, wk):
        return jnp.where(same_head, _dot_tn((k * wk).astype(BF16), v.astype(BF16)), 0.0)

    def chunk_out(q, k, v, gate, s_fwd, s_bwd):
        o = _dot((q * wqf).astype(BF16), s_fwd.astype(BF16)) + _dot((q * wqb).astype(BF16), s_bwd.astype(BF16))
        kb = k.astype(BF16)
        for h in range(4):
            sel = lane == h
            s = _dot_nt(jnp.where(sel, q, 0.0).astype(BF16), kb)
            o = o + _dot((s * decay[h]).astype(BF16), jnp.where(sel, v, 0.0).astype(BF16))
        on = o * lax.rsqrt(_group_sums(o * o, ones) * (1.0 / HEAD_DIM) + EPS) * rg
        return (on * _silu(gate)).astype(BF16)

    _, kc, vc, _ = load(uc_ref, 0)
    s_fwd = increment(kc, vc, wkf)
    s_bwd = increment(kc, vc, wkb)

    for j in range(n_chunks - 1, -1, -1):
        tst_ref[j] = s_bwd
        if j > 0:
            _, k, v, _ = load(ul_ref, j * c)
            s_bwd = s_bwd * dcb + increment(k, v, wkb)

    for j in range(n_chunks):
        q, k, v, gate = load(ul_ref, j * c)
        ydl_ref[pl.ds(j * c, c), :] = chunk_out(q, k, v, gate, s_fwd, tst_ref[j])
        if j < n_chunks - 1:
            s_fwd = s_fwd * dcf + increment(k, v, wkf)

    if with_ctx:
        qc, kc, vc, gc = load(uc_ref, 0)
        zeros = jnp.zeros((256, 256), F32)
        ydc_ref[...] = chunk_out(qc, kc, vc, gc, zeros, zeros)


def _mixer(ucd, decay_lanes, decay_heads, ret_gain_lanes, pool_w_bd, pool_scale, *, with_ctx):
    ctx_blk0 = T_LAT // CTX_LEN
    const = lambda b: (0, 0)
    out_specs = [pl.BlockSpec((SEQ, 256), lambda b: (b, 0))] * 2
    out_shape = [jax.ShapeDtypeStruct((T_LAT, 256), BF16)] * 2
    if with_ctx:
        out_specs = out_specs + [pl.BlockSpec((CTX_LEN, 256), lambda b: (b, 0))] * 2
        out_shape = out_shape + [jax.ShapeDtypeStruct((T_CTX, 256), BF16)] * 2
    return pl.pallas_call(
        functools.partial(_mix_kernel, with_ctx=with_ctx),
        grid=(BATCH,),
        in_specs=[
            pl.BlockSpec((SEQ, 1280), lambda b: (b, 0)),
            pl.BlockSpec((CTX_LEN, 1280), lambda b: (ctx_blk0 + b, 0)),
            pl.BlockSpec((2, 256), const),
            pl.BlockSpec((8, 128), const),
            pl.BlockSpec((1, 256), const),
            pl.BlockSpec((256, 256), const),
            pl.BlockSpec((1, 256), const),
        ],
        out_specs=out_specs,
        out_shape=out_shape,
        scratch_shapes=[pltpu.VMEM((SEQ // RET_C, 256, 256), F32)],
        compiler_params=_cparams(("parallel",)),
        name="pool_retention_mixer",
    )(ucd, ucd, decay_lanes, decay_heads, ret_gain_lanes, pool_w_bd, pool_scale)


def _outproj_kernel(*refs, n_y, lat_tiles):
    x_ref, m_ref, g_ref = refs[:3]
    y_refs = refs[3:3 + 4 * n_y]
    w_ref, o_ref = refs[3 + 4 * n_y:]
    rows = pl.ds(0, o_ref.shape[0])
    y = None
    for j in range(4):
        part = _dot(_token_tile(y_refs[n_y * j:n_y * (j + 1)], rows, lat_tiles), w_ref[256 * j:256 * (j + 1), :])
        y = part if y is None else y + part
    yn = y * _rms_rows(y) * g_ref[3:4, :]
    o_ref[...] = x_ref[...] + m_ref[5:6, :] * yn


def _out_projection(h, mod, gains, ys, w_out, *, layer, rows):
    tm = TM_PROJ
    n_y = len(ys[0])
    y_specs, y_args = [], []
    for parts in ys:
        y_specs += _stream_specs(parts, tm, 256)
        y_args += list(parts)
    return pl.pallas_call(
        functools.partial(_outproj_kernel, n_y=n_y, lat_tiles=T_LAT // tm),
        grid=(rows // tm,),
        in_specs=[
            pl.BlockSpec((tm, D_MODEL), lambda i: (i, 0)),
            pl.BlockSpec((None, 9, D_MODEL), lambda i: (_mod_row_index(i, tm), 0, 0)),
            pl.BlockSpec((6, D_MODEL), lambda i: (0, 0)),
        ] + y_specs + [
            pl.BlockSpec((None, D_MODEL, D_MODEL), lambda i: (layer, 0, 0), pipeline_mode=pl.Buffered(1)),
        ],
        out_specs=pl.BlockSpec((tm, D_MODEL), lambda i: (i, 0)),
        out_shape=jax.ShapeDtypeStruct((rows, D_MODEL), F32),
        compiler_params=_cparams(("parallel",)),
        name="out_projection",
    )(h, mod, gains, *y_args, w_out)


def _rope_tables(dim, tm):
    q = dim // 4
    n_rows = SEQ // GRID_W
    rows = jnp.repeat(jnp.arange(n_rows), GRID_W).astype(F32)
    cols = jnp.tile(jnp.arange(GRID_W), n_rows).astype(F32)
    inv = ROPE_BASE ** (-jnp.arange(q, dtype=F32) / q)
    ar, ac = rows[:, None] * inv, cols[:, None] * inv
    cos = jnp.concatenate([jnp.cos(ar)] * 2 + [jnp.cos(ac)] * 2, axis=-1)
    sin = jnp.concatenate([jnp.sin(ar)] * 2 + [jnp.sin(ac)] * 2, axis=-1)
    first_half = jnp.tile(jnp.repeat(jnp.array([True, False]), q), 2)
    tabs = jnp.stack([cos, jnp.where(first_half, -sin, 0.0), jnp.where(first_half, 0.0, sin)])
    tabs = jnp.tile(tabs, (1, 1, 128 // dim))
    ident = jnp.stack([jnp.ones((tm, 128), F32), jnp.zeros((tm, 128), F32), jnp.zeros((tm, 128), F32)])
    return jnp.concatenate([tabs, ident], axis=1)


def _block_diag(blocks):
    n, d, _ = blocks.shape
    eye = jnp.eye(n, dtype=blocks.dtype)
    return jnp.einsum('gcd,gh->gchd', blocks, eye).reshape(n * d, n * d)


_GQA_HEAD_ORDER = (0, 2, 1, 3)


def kernel(x, c, ctx, c_ctx, w_mod, b_mod, norm_gain, ffn_w_gate, ffn_w_up, ffn_w_down, w_in, w_out,
           attn_qk_gain, diff_lambda, diff_out_gain, pool_w, pool_scale, ret_decay_logit, ret_out_gain):
    xs = (x.reshape(T_LAT, D_MODEL), ctx.reshape(T_CTX, D_MODEL))
    cc = jnp.concatenate([c, c_ctx[None, :], jnp.zeros((MOD_ROWS - BATCH - 1, D_MODEL), F32)], axis=0)
    mod_all = _modulation(cc, w_mod, b_mod).reshape(DEPTH, MOD_ROWS, 9, D_MODEL)
    tab_a = _rope_tables(HEAD_DIM, TM_PROJ)
    tab_b = _rope_tables(DIFF_QK_DIM, TM_PROJ)
    head_perm = jnp.concatenate([jnp.arange(HEAD_DIM) + HEAD_DIM * hd for hd in _GQA_HEAD_ORDER]
                                + [jnp.arange(256, D_IN)])
    w_in_b = w_in.astype(BF16)[:, :, head_perm]
    w_out_b = w_out.astype(BF16)[:, head_perm[:D_MODEL], :]
    wg_b, wu_b, wd_b = ffn_w_gate.astype(BF16), ffn_w_up.astype(BF16), ffn_w_down.astype(BF16)

    for i in range(DEPTH):
        last = i == DEPTH - 1
        mod = mod_all[i]
        gains = norm_gain[i]
        qk_gain_lanes = jnp.tile(attn_qk_gain[i], (1, 256 // HEAD_DIM))
        diff_gain_lanes = jnp.tile(diff_out_gain[i], 256 // HEAD_DIM)[None, :]
        ret_gain_lanes = jnp.tile(ret_out_gain[i], 256 // HEAD_DIM)[None, :]
        decay_lanes = jnp.repeat(ret_decay_logit[i], HEAD_DIM, axis=1)
        decay_heads = jnp.broadcast_to(ret_decay_logit[i].reshape(8, 1), (8, 128))
        pool_w_bd = _block_diag(pool_w[i]).astype(BF16)
        lam_init = 0.8 - 0.6 * math.exp(-0.3 * i)

        def ffn(hs, sub, rows):
            return _ffn_sublayer(hs, mod, gains, wg_b, wu_b, wd_b, layer=i, sub=sub, rows=rows)

        h = ffn(xs, 0, T_ALL)
        qa, ka, va, qb, kb, vb, ucd = _in_projection(h, mod, gains, w_in_b, tab_a, tab_b, qk_gain_lanes, layer=i)
        ya, yb = _attention(qa, ka, va, qb, kb, vb, diff_lambda[i], diff_gain_lanes, lam_init=lam_init, latent=True)
        mixed = _mixer(ucd, decay_lanes, decay_heads, ret_gain_lanes, pool_w_bd, pool_scale[i][None, :],
                       with_ctx=not last)
        if last:
            ys, rows = [(ya,), (yb,), (mixed[0],), (mixed[1],)], T_LAT
        else:
            ya_c, yb_c = _attention(qa, ka, va, qb, kb, vb, diff_lambda[i], diff_gain_lanes, lam_init=lam_init,
                                    latent=False)
            ys, rows = [(ya, ya_c), (yb, yb_c), (mixed[0], mixed[2]), (mixed[1], mixed[3])], T_ALL
        h = _out_projection(h, mod, gains, ys, w_out_b, layer=i, rows=rows)
        h = ffn((h,), 2, rows)
        xs = (h,)
    return h.reshape(BATCH, SEQ, D_MODEL)
```

```python
import functools
import math

import jax
import jax.numpy as jnp
from jax import lax
from jax.experimental import pallas as pl
from jax.experimental.pallas import tpu as pltpu

F32 = jnp.float32
BF16 = jnp.bfloat16

D_MODEL = 1024
BATCH = 8
SEQ = 2048
DEPTH = 2
CTX_LEN = 256
GRID_W = 64
HEAD_DIM = 64
DIFF_QK_DIM = 32
GROUP_WIDTH = 256
D_FF = 2816
D_IN = 2560
FFN_RESIDUAL = 0.5
ROPE_BASE = 10000.0
EPS = 1e-6

T_LAT = BATCH * SEQ
T_CTX = BATCH * CTX_LEN
T_ALL = T_LAT + T_CTX
MOD_ROWS = 16

TM_FFN = 512
FFN_SPLIT = 2
TM_PROJ = 512
LOG2E = 1.4426950408889634
TQ = 256
RET_C = 256
VMEM_LIMIT = 56 * 1024 * 1024


def _cparams(sem):
    return pltpu.CompilerParams(dimension_semantics=sem, vmem_limit_bytes=VMEM_LIMIT)


def _silu(x):
    return x * jax.nn.sigmoid(x)


def _dot(a, b):
    return jnp.dot(a, b, preferred_element_type=F32)


def _dot_nt(a, b):
    return lax.dot_general(a, b, (((1,), (1,)), ((), ())), preferred_element_type=F32)


def _dot_tn(a, b):
    return lax.dot_general(a, b, (((0,), (0,)), ((), ())), preferred_element_type=F32)


def _block_ones(n, blk):
    r = lax.broadcasted_iota(jnp.int32, (n, n), 0) // blk
    c = lax.broadcasted_iota(jnp.int32, (n, n), 1) // blk
    return r == c


def _group_sums(sq, ones_bf16):
    hi = sq.astype(BF16)
    lo = (sq - hi.astype(F32)).astype(BF16)
    return _dot(hi, ones_bf16) + _dot(lo, ones_bf16)


def _rms_rows(x):
    return lax.rsqrt(jnp.mean(x * x, axis=-1, keepdims=True) + EPS)


def _mod_row_index(tile, tm):
    return jnp.minimum((tile * tm) // SEQ, BATCH)


def _mod_kernel(cc_ref, w_ref, b_ref, o_ref):
    a = _silu(cc_ref[...]).astype(BF16)
    o_ref[...] = _dot(a, w_ref[...].astype(BF16)) + b_ref[...]


def _modulation(cc, w_mod, b_mod):
    tn = 1024
    n_out = 9 * D_MODEL
    return pl.pallas_call(
        _mod_kernel,
        grid=(DEPTH, n_out // tn),
        in_specs=[
            pl.BlockSpec((MOD_ROWS, D_MODEL), lambda l, j: (0, 0)),
            pl.BlockSpec((None, D_MODEL, tn), lambda l, j: (l, 0, j)),
            pl.BlockSpec((None, 1, tn), lambda l, j: (l, 0, j)),
        ],
        out_specs=pl.BlockSpec((None, MOD_ROWS, tn), lambda l, j: (l, 0, j)),
        out_shape=jax.ShapeDtypeStruct((DEPTH, MOD_ROWS, n_out), F32),
        compiler_params=_cparams(("parallel", "parallel")),
        name="adaln_modulation",
    )(cc, w_mod, b_mod.reshape(DEPTH, 1, n_out))


def _token_tile(refs, rows, lat_tiles):
    if len(refs) == 1:
        return refs[0][rows, :]
    return jnp.where(pl.program_id(0) < lat_tiles, refs[0][rows, :], refs[1][rows, :])


def _ffn_kernel(*refs, n_x, n_y, sub, g_pre, g_post, lat_tiles):
    x_refs = refs[:n_x]
    m_ref, g_ref = refs[n_x:n_x + 2]
    y_refs = refs[n_x + 2:n_x + 2 + 4 * n_y]
    wo_ref = refs[n_x + 2 + 4 * n_y] if n_y else None
    wg_ref, wu_ref, wd_ref, o_ref = refs[-4:]
    pre = g_ref[g_pre:g_pre + 1, :] * (1.0 + m_ref[3 * sub + 1:3 * sub + 2, :])
    shift = m_ref[3 * sub:3 * sub + 1, :]
    post = (FFN_RESIDUAL * m_ref[3 * sub + 2:3 * sub + 3, :]) * g_ref[g_post:g_post + 1, :]
    mix_post = m_ref[5:6, :] * g_ref[3:4, :]
    rs = o_ref.shape[0] // FFN_SPLIT
    for s in range(FFN_SPLIT):
        rows = pl.ds(s * rs, rs)
        x = _token_tile(x_refs, rows, lat_tiles)
        if n_y:
            y = None
            for j in range(4):
                part = _dot(_token_tile(y_refs[n_y * j:n_y * (j + 1)], rows, lat_tiles),
                            wo_ref[256 * j:256 * (j + 1), :])
                y = part if y is None else y + part
            x = x + (y * _rms_rows(y)) * mix_post
        hm = (x * _rms_rows(x) * pre + shift).astype(BF16)
        gate = _dot(hm, wg_ref[...])
        up = _dot(hm, wu_ref[...])
        y = _dot((_silu(gate) * up).astype(BF16), wd_ref[...])
        o_ref[rows, :] = x + (y * _rms_rows(y)) * post


def _stream_specs(xs, tm, width):
    if len(xs) == 1:
        return [pl.BlockSpec((tm, width), lambda i: (i, 0))]
    lat_tiles = xs[0].shape[0] // tm
    return [pl.BlockSpec((tm, width), lambda i: (jnp.minimum(i, lat_tiles - 1), 0)),
            pl.BlockSpec((tm, width), lambda i: (jnp.maximum(i - lat_tiles, 0), 0))]


def _ffn_sublayer(xs, mod, gains, wg, wu, wd, *, layer, sub, rows, ys=(), w_out=None):
    tm = TM_FFN
    n_y = len(ys[0]) if ys else 0
    kern = functools.partial(_ffn_kernel, n_x=len(xs), n_y=n_y, sub=sub, g_pre=2 * sub, g_post=2 * sub + 1,
                             lat_tiles=T_LAT // tm)
    k = sub // 2
    resident = dict(pipeline_mode=pl.Buffered(1))
    y_specs, y_args = [], []
    for parts in ys:
        y_specs += _stream_specs(parts, tm, 256)
        y_args += list(parts)
    if ys:
        y_specs.append(pl.BlockSpec((None, D_MODEL, D_MODEL), lambda i: (layer, 0, 0), **resident))
        y_args.append(w_out)
    return pl.pallas_call(
        kern,
        grid=(rows // tm,),
        in_specs=_stream_specs(xs, tm, D_MODEL) + [
            pl.BlockSpec((None, 9, D_MODEL), lambda i: (_mod_row_index(i, tm), 0, 0)),
            pl.BlockSpec((6, D_MODEL), lambda i: (0, 0)),
        ] + y_specs + [
            pl.BlockSpec((None, None, D_MODEL, D_FF), lambda i: (layer, k, 0, 0), **resident),
            pl.BlockSpec((None, None, D_MODEL, D_FF), lambda i: (layer, k, 0, 0), **resident),
            pl.BlockSpec((None, None, D_FF, D_MODEL), lambda i: (layer, k, 0, 0), **resident),
        ],
        out_specs=pl.BlockSpec((tm, D_MODEL), lambda i: (i, 0)),
        out_shape=jax.ShapeDtypeStruct((rows, D_MODEL), F32),
        compiler_params=_cparams(("parallel",)),
        name=f"ffn_sublayer_{sub}",
    )(*xs, mod, gains, *y_args, wg, wu, wd)


def _rope(x, tab_ref, rows, quarter):
    w = x.shape[1]
    reps = w // tab_ref.shape[2]

    def tab(k):
        t = tab_ref[k, rows, :]
        return t if reps == 1 else jnp.concatenate([t] * reps, axis=1)

    return x * tab(0) + pltpu.roll(x, w - quarter, 1) * tab(1) + pltpu.roll(x, quarter, 1) * tab(2)


def _inproj_kernel(x_ref, m_ref, g_ref, w_ref, ta_ref, tb_ref, qkg_ref,
                   qa_ref, ka_ref, va_ref, qb_ref, kb_ref, vb_ref, ucd_ref):
    pre = g_ref[2:3, :] * (1.0 + m_ref[4:5, :])
    shift = m_ref[3:4, :]
    ones = _block_ones(256, HEAD_DIM).astype(BF16)
    inv_d = 1.0 / HEAD_DIM
    lane128 = lax.broadcasted_iota(jnp.int32, (1, 128), 1)
    rs = x_ref.shape[0] // 2
    for s in range(2):
        rows = pl.ds(s * rs, rs)
        x = x_ref[rows, :]
        hm = (x * _rms_rows(x) * pre + shift).astype(BF16)

        def proj(lo, hi):
            return _dot(hm, w_ref[:, lo:hi])

        q = proj(0, 256)
        q = q * lax.rsqrt(_group_sums(q * q, ones) * inv_d + EPS) * qkg_ref[0:1, :]
        qa_ref[rows, :] = (_rope(q, ta_ref, rows, HEAD_DIM // 4) * (HEAD_DIM ** -0.5 * LOG2E)).astype(BF16)
        k = proj(256, 384)
        k = k * lax.rsqrt(_group_sums(k * k, ones[:128, :128]) * inv_d + EPS) * qkg_ref[1:2, :128]
        ka_ref[rows, :] = _rope(k, ta_ref, rows, HEAD_DIM // 4).astype(BF16)
        v = proj(384, 512)
        for g in range(2):
            va_ref[rows, 128 * g:128 * g + 128] = jnp.where(lane128 == HEAD_DIM * (1 - g), 1.0, v).astype(BF16)

        qb_ref[rows, :] = (_rope(proj(512, 768), tb_ref, rows, DIFF_QK_DIM // 4)
                           * (DIFF_QK_DIM ** -0.5 * LOG2E)).astype(BF16)
        kb_ref[rows, :] = _rope(proj(768, 1024), tb_ref, rows, DIFF_QK_DIM // 4).astype(BF16)
        v = proj(1024, 1280)
        for hd in range(4):
            own = (lane128 // HEAD_DIM) == hd % 2
            ones_col = jnp.where(lane128 == HEAD_DIM * (1 - hd % 2), 1.0, 0.0)
            pair = v[:, 128 * (hd // 2):128 * (hd // 2) + 128]
            vb_ref[rows, 128 * hd:128 * hd + 128] = jnp.where(own, pair, ones_col).astype(BF16)

        ucd_ref[rows, :] = proj(1280, 2560)


def _in_projection(h, mod, gains, w_in, tab_a, tab_b, qk_gain_lanes, *, layer):
    tm = TM_PROJ
    lat_tiles = T_LAT // tm
    tiles_per_seq = SEQ // tm

    def tab_idx(i):
        return jnp.where(i < lat_tiles, i % tiles_per_seq, tiles_per_seq)

    def row_spec(width):
        return pl.BlockSpec((tm, width), lambda i: (i, 0))

    def out(width, dtype):
        return jax.ShapeDtypeStruct((T_ALL, width), dtype)

    return pl.pallas_call(
        _inproj_kernel,
        grid=(T_ALL // tm,),
        in_specs=[
            row_spec(D_MODEL),
            pl.BlockSpec((None, 9, D_MODEL), lambda i: (_mod_row_index(i, tm), 0, 0)),
            pl.BlockSpec((6, D_MODEL), lambda i: (0, 0)),
            pl.BlockSpec((None, D_MODEL, D_IN), lambda i: (layer, 0, 0), pipeline_mode=pl.Buffered(1)),
            pl.BlockSpec((3, tm, 128), lambda i: (0, tab_idx(i), 0)),
            pl.BlockSpec((3, tm, 128), lambda i: (0, tab_idx(i), 0)),
            pl.BlockSpec((2, 256), lambda i: (0, 0)),
        ],
        out_specs=[row_spec(256), row_spec(128), row_spec(256), row_spec(256), row_spec(256), row_spec(512),
                   row_spec(1280)],
        out_shape=[out(256, BF16), out(128, BF16), out(256, BF16), out(256, BF16), out(256, BF16), out(512, BF16),
                   out(1280, F32)],
        compiler_params=_cparams(("parallel",)),
        name="in_projection",
    )(h, mod, gains, w_in, tab_a, tab_b, qk_gain_lanes)


def _softmax_pv(qs, k_refs, v_refs, lanes, ones_lane):
    ss = [_dot_nt(qs, k[...]) for k in k_refs]
    m = ss[0].max(axis=-1, keepdims=True)
    for s in ss[1:]:
        m = jnp.maximum(m, s.max(axis=-1, keepdims=True))
    ps = [jnp.exp2(s - m).astype(BF16) for s in ss]
    half = qs.shape[0] // 2
    outs = []
    for r0 in (0, half):
        o = None
        for p, v in zip(ps, v_refs):
            pv = _dot(p[r0:r0 + half], v[:, lanes])
            o = pv if o is None else o + pv
        outs.append(o / o[:, ones_lane:ones_lane + 1])
    return outs


def _attn_kernel(*refs, n_parts, lam_init):
    qa_ref = refs[0]
    ka = refs[1:1 + n_parts]
    va = refs[1 + n_parts:1 + 2 * n_parts]
    qb_ref = refs[1 + 2 * n_parts]
    kb = refs[2 + 2 * n_parts:2 + 3 * n_parts]
    vb = refs[2 + 3 * n_parts:2 + 4 * n_parts]
    lam_ref, dg_ref, ya_ref, yb_ref = refs[2 + 4 * n_parts:]

    lane128 = lax.broadcasted_iota(jnp.int32, (1, 128), 1)
    first = lane128 < HEAD_DIM
    qa = qa_ref[...]
    zero = jnp.zeros((), BF16)
    outs = []
    for g in range(2):
        sel = (lane128 // HEAD_DIM) == g
        qs = jnp.concatenate([jnp.where(sel, qa[:, 0:128], zero), jnp.where(sel, qa[:, 128:256], zero)], axis=0)
        outs.append(_softmax_pv(qs, ka, va, slice(128 * g, 128 * g + 128), HEAD_DIM * (1 - g)))
    y0 = jnp.where(first, outs[0][0], outs[1][0])
    y1 = jnp.where(first, outs[0][1], outs[1][1])
    ya_ref[...] = jnp.concatenate([y0, y1], axis=1).astype(BF16)

    lp = lam_ref[...]
    lam = (jnp.exp(jnp.sum(lp[0:1] * lp[1:2], axis=-1, keepdims=True))
           - jnp.exp(jnp.sum(lp[2:3] * lp[3:4], axis=-1, keepdims=True)) + lam_init)
    half = lax.broadcasted_iota(jnp.int32, (1, 256), 1) // DIFF_QK_DIM
    qb = qb_ref[...]
    d = []
    for h in range(4):
        qs = jnp.concatenate([jnp.where(half == 2 * h, qb, zero), jnp.where(half == 2 * h + 1, qb, zero)], axis=0)
        o1, o2 = _softmax_pv(qs, kb, vb, slice(128 * h, 128 * h + 128), HEAD_DIM * (1 - h % 2))
        d.append(o1 - lam * o2)
    y = jnp.concatenate([jnp.where(first, d[0], d[1]), jnp.where(first, d[2], d[3])], axis=1)
    ones = _block_ones(256, HEAD_DIM).astype(BF16)
    yn = y * lax.rsqrt(_group_sums(y * y, ones) * (1.0 / HEAD_DIM) + EPS) * dg_ref[...]
    yb_ref[...] = (yn * (1.0 - lam_init)).astype(BF16)


def _attention(qa, ka, va, qb, kb, vb, diff_lambda, diff_gain_lanes, *, lam_init, latent):
    ctx_blk0 = T_LAT // CTX_LEN
    if latent:
        grid = (BATCH, SEQ // TQ)
        q_map = lambda b, i: (b * (SEQ // TQ) + i, 0)
        kv_specs = lambda w: [pl.BlockSpec((CTX_LEN, w), lambda b, i: (ctx_blk0 + b, 0)),
                              pl.BlockSpec((SEQ, w), lambda b, i: (b, 0))]
        rows = T_LAT
        n_parts = 2
        const = lambda b, i: (0, 0)
        o_map = q_map
    else:
        grid = (BATCH,)
        q_map = lambda b: (ctx_blk0 + b, 0)
        kv_specs = lambda w: [pl.BlockSpec((CTX_LEN, w), lambda b: (ctx_blk0 + b, 0))]
        rows = T_CTX
        n_parts = 1
        const = lambda b: (0, 0)
        o_map = lambda b: (b, 0)
    in_specs = ([pl.BlockSpec((TQ, 256), q_map)] + kv_specs(128) + kv_specs(256)
                + [pl.BlockSpec((TQ, 256), q_map)] + kv_specs(256) + kv_specs(512)
                + [pl.BlockSpec((4, DIFF_QK_DIM), const), pl.BlockSpec((1, 256), const)])
    args = [qa] + [ka] * n_parts + [va] * n_parts + [qb] + [kb] * n_parts + [vb] * n_parts
    return pl.pallas_call(
        functools.partial(_attn_kernel, n_parts=n_parts, lam_init=lam_init),
        grid=grid,
        in_specs=in_specs,
        out_specs=[pl.BlockSpec((TQ, 256), o_map), pl.BlockSpec((TQ, 256), o_map)],
        out_shape=[jax.ShapeDtypeStruct((rows, 256), BF16), jax.ShapeDtypeStruct((rows, 256), BF16)],
        compiler_params=_cparams(("parallel",) * len(grid)),
        name="attention_latent" if latent else "attention_context",
    )(*args, diff_lambda, diff_gain_lanes)


def _pool(u, pw, ps):
    n_seq = u.shape[0]
    pad = 16
    n = n_seq + 2 * pad
    z = jnp.zeros((pad, 256), F32)
    p1 = jnp.concatenate([z, u, z], axis=0)
    s2 = p1 + pltpu.roll(p1, n - 1, 0)
    s4 = s2 + pltpu.roll(s2, n - 2, 0)
    s8 = s4 + pltpu.roll(s4, n - 4, 0)
    s16 = s8 + pltpu.roll(s8, n - 8, 0)
    w2 = pltpu.roll(s2, 1, 0)[pad:pad + n_seq]
    w4 = pltpu.roll(s4, 2, 0)[pad:pad + n_seq]
    w8 = pltpu.roll(s8, 4, 0)[pad:pad + n_seq]
    w16 = pltpu.roll(s16, 8, 0)[pad:pad + n_seq]
    grp = lax.broadcasted_iota(jnp.int32, (1, 256), 1) // 64
    wsum = jnp.where(grp == 0, w2, jnp.where(grp == 1, w4, jnp.where(grp == 2, w8, w16)))
    back = jnp.where(grp == 0, 1, jnp.where(grp == 1, 2, jnp.where(grp == 2, 4, 8)))
    t = lax.broadcasted_iota(jnp.int32, (n_seq, 256), 0)
    count = jnp.minimum(t + (back - 1), n_seq - 1) - jnp.maximum(t - back, 0) + 1
    pooled = wsum / count.astype(F32) - u
    return _dot(pooled.astype(BF16), pw) * ps


def _log_sigmoid(x):
    return jnp.minimum(x, 0.0) - jnp.log(1.0 + jnp.exp(-jnp.abs(x)))


def _mix_kernel(*refs, with_ctx):
    if with_ctx:
        (ul_ref, uc_ref, dl_ref, dh_ref, rg_ref, pw_ref, ps_ref,
         ycl_ref, ydl_ref, ycc_ref, ydc_ref, tst_ref) = refs
    else:
        (ul_ref, uc_ref, dl_ref, dh_ref, rg_ref, pw_ref, ps_ref, ycl_ref, ydl_ref, tst_ref) = refs
    c = RET_C
    n_chunks = SEQ // c
    pw = pw_ref[...]
    ps = ps_ref[...]

    ycl_ref[...] = _pool(ul_ref[:, 0:256], pw, ps).astype(BF16)
    if with_ctx:
        ycc_ref[...] = _pool(uc_ref[:, 0:256], pw, ps).astype(BF16)

    lg = _log_sigmoid(dl_ref[...])
    lgh = _log_sigmoid(dh_ref[...])
    lgf, lgb = lg[0:1, :], lg[1:2, :]
    pos = lax.broadcasted_iota(jnp.int32, (c, 1), 0).astype(F32)
    wkf = jnp.exp((c - 1.0 - pos) * lgf)
    wkb = jnp.exp(pos * lgb)
    wqf = jnp.exp((pos + 1.0) * lgf)
    wqb = jnp.exp((c - pos) * lgb)
    dcf = jnp.exp(c * lgf)
    dcb = jnp.exp(c * lgb)
    same_head = _block_ones(256, HEAD_DIM)
    ones = same_head.astype(BF16)
    lane = lax.broadcasted_iota(jnp.int32, (1, 256), 1) // HEAD_DIM
    diff = (lax.broadcasted_iota(jnp.int32, (c, c), 0) - lax.broadcasted_iota(jnp.int32, (c, c), 1)).astype(F32)
    decay = []
    for h in range(4):
        df = jnp.exp(jnp.maximum(diff, 0.0) * lgh[h:h + 1, 0:1])
        db = jnp.exp(jnp.maximum(-diff, 0.0) * lgh[4 + h:5 + h, 0:1])
        decay.append(jnp.where(diff > 0, df, jnp.where(diff < 0, db, 2.0)))
    rg = rg_ref[...]
    k_scale = HEAD_DIM ** -0.5

    def load(ref, r0):
        rows = pl.ds(r0, c)
        return ref[rows, 256:512], ref[rows, 512:768] * k_scale, ref[rows, 768:1024], ref[rows, 1024:1280]

    def increment(k, v, wk):
        return jnp.where(same_head, _dot_tn((k * wk).astype(BF16), v.astype(BF16)), 0.0)

    def chunk_out(q, k, v, gate, s_fwd, s_bwd):
        o = _dot((q * wqf).astype(BF16), s_fwd.astype(BF16)) + _dot((q * wqb).astype(BF16), s_bwd.astype(BF16))
        kb = k.astype(BF16)
        for h in range(4):
            sel = lane == h
            s = _dot_nt(jnp.where(sel, q, 0.0).astype(BF16), kb)
            o = o + _dot((s * decay[h]).astype(BF16), jnp.where(sel, v, 0.0).astype(BF16))
        on = o * lax.rsqrt(_group_sums(o * o, ones) * (1.0 / HEAD_DIM) + EPS) * rg
        return (on * _silu(gate)).astype(BF16)

    _, kc, vc, _ = load(uc_ref, 0)
    s_fwd = increment(kc, vc, wkf)
    s_bwd = increment(kc, vc, wkb)

    for j in range(n_chunks - 1, -1, -1):
        tst_ref[j] = s_bwd
        if j > 0:
            _, k, v, _ = load(ul_ref, j * c)
            s_bwd = s_bwd * dcb + increment(k, v, wkb)

    for j in range(n_chunks):
        q, k, v, gate = load(ul_ref, j * c)
        ydl_ref[pl.ds(j * c, c), :] = chunk_out(q, k, v, gate, s_fwd, tst_ref[j])
        if j < n_chunks - 1:
            s_fwd = s_fwd * dcf + increment(k, v, wkf)

    if with_ctx:
        qc, kc, vc, gc = load(uc_ref, 0)
        zeros = jnp.zeros((256, 256), F32)
        ydc_ref[...] = chunk_out(qc, kc, vc, gc, zeros, zeros)


def _mixer(ucd, decay_lanes, decay_heads, ret_gain_lanes, pool_w_bd, pool_scale, *, with_ctx):
    ctx_blk0 = T_LAT // CTX_LEN
    const = lambda b: (0, 0)
    out_specs = [pl.BlockSpec((SEQ, 256), lambda b: (b, 0))] * 2
    out_shape = [jax.ShapeDtypeStruct((T_LAT, 256), BF16)] * 2
    if with_ctx:
        out_specs = out_specs + [pl.BlockSpec((CTX_LEN, 256), lambda b: (b, 0))] * 2
        out_shape = out_shape + [jax.ShapeDtypeStruct((T_CTX, 256), BF16)] * 2
    return pl.pallas_call(
        functools.partial(_mix_kernel, with_ctx=with_ctx),
        grid=(BATCH,),
        in_specs=[
            pl.BlockSpec((SEQ, 1280), lambda b: (b, 0)),
            pl.BlockSpec((CTX_LEN, 1280), lambda b: (ctx_blk0 + b, 0)),
            pl.BlockSpec((2, 256), const),
            pl.BlockSpec((8, 128), const),
            pl.BlockSpec((1, 256), const),
            pl.BlockSpec((256, 256), const),
            pl.BlockSpec((1, 256), const),
        ],
        out_specs=out_specs,
        out_shape=out_shape,
        scratch_shapes=[pltpu.VMEM((SEQ // RET_C, 256, 256), F32)],
        compiler_params=_cparams(("parallel",)),
        name="pool_retention_mixer",
    )(ucd, ucd, decay_lanes, decay_heads, ret_gain_lanes, pool_w_bd, pool_scale)


def _rope_tables(dim, tm):
    q = dim // 4
    n_rows = SEQ // GRID_W
    rows = jnp.repeat(jnp.arange(n_rows), GRID_W).astype(F32)
    cols = jnp.tile(jnp.arange(GRID_W), n_rows).astype(F32)
    inv = ROPE_BASE ** (-jnp.arange(q, dtype=F32) / q)
    ar, ac = rows[:, None] * inv, cols[:, None] * inv
    cos = jnp.concatenate([jnp.cos(ar)] * 2 + [jnp.cos(ac)] * 2, axis=-1)
    sin = jnp.concatenate([jnp.sin(ar)] * 2 + [jnp.sin(ac)] * 2, axis=-1)
    first_half = jnp.tile(jnp.repeat(jnp.array([True, False]), q), 2)
    tabs = jnp.stack([cos, jnp.where(first_half, -sin, 0.0), jnp.where(first_half, 0.0, sin)])
    tabs = jnp.tile(tabs, (1, 1, 128 // dim))
    ident = jnp.stack([jnp.ones((tm, 128), F32), jnp.zeros((tm, 128), F32), jnp.zeros((tm, 128), F32)])
    return jnp.concatenate([tabs, ident], axis=1)


def _block_diag(blocks):
    n, d, _ = blocks.shape
    eye = jnp.eye(n, dtype=blocks.dtype)
    return jnp.einsum('gcd,gh->gchd', blocks, eye).reshape(n * d, n * d)


_GQA_HEAD_ORDER = (0, 2, 1, 3)


def _permute_gqa_heads(w, axis):
    parts = [lax.slice_in_dim(w, HEAD_DIM * hd, HEAD_DIM * (hd + 1), axis=axis) for hd in _GQA_HEAD_ORDER]
    rest = lax.slice_in_dim(w, HEAD_DIM * len(_GQA_HEAD_ORDER), w.shape[axis], axis=axis)
    return jnp.concatenate(parts + [rest], axis=axis)


def kernel(x, c, ctx, c_ctx, w_mod, b_mod, norm_gain, ffn_w_gate, ffn_w_up, ffn_w_down, w_in, w_out,
           attn_qk_gain, diff_lambda, diff_out_gain, pool_w, pool_scale, ret_decay_logit, ret_out_gain):
    xs = (x.reshape(T_LAT, D_MODEL), ctx.reshape(T_CTX, D_MODEL))
    cc = jnp.concatenate([c, c_ctx[None, :], jnp.zeros((MOD_ROWS - BATCH - 1, D_MODEL), F32)], axis=0)
    mod_all = _modulation(cc, w_mod, b_mod).reshape(DEPTH, MOD_ROWS, 9, D_MODEL)
    tab_a = _rope_tables(HEAD_DIM, TM_PROJ)
    tab_b = _rope_tables(DIFF_QK_DIM, TM_PROJ)
    w_in_b = _permute_gqa_heads(w_in, 2).astype(BF16)
    w_out_b = _permute_gqa_heads(w_out, 1).astype(BF16)
    wg_b, wu_b, wd_b = ffn_w_gate.astype(BF16), ffn_w_up.astype(BF16), ffn_w_down.astype(BF16)

    for i in range(DEPTH):
        last = i == DEPTH - 1
        mod = mod_all[i]
        gains = norm_gain[i]
        qk_gain_lanes = jnp.tile(attn_qk_gain[i], (1, 256 // HEAD_DIM))
        diff_gain_lanes = jnp.tile(diff_out_gain[i], 256 // HEAD_DIM)[None, :]
        ret_gain_lanes = jnp.tile(ret_out_gain[i], 256 // HEAD_DIM)[None, :]
        decay_lanes = jnp.repeat(ret_decay_logit[i], HEAD_DIM, axis=1)
        decay_heads = jnp.broadcast_to(ret_decay_logit[i].reshape(8, 1), (8, 128))
        pool_w_bd = _block_diag(pool_w[i]).astype(BF16)
        lam_init = 0.8 - 0.6 * math.exp(-0.3 * i)

        def ffn(hs, sub, rows, **mix):
            return _ffn_sublayer(hs, mod, gains, wg_b, wu_b, wd_b, layer=i, sub=sub, rows=rows, **mix)

        h = ffn(xs, 0, T_ALL)
        qa, ka, va, qb, kb, vb, ucd = _in_projection(h, mod, gains, w_in_b, tab_a, tab_b, qk_gain_lanes, layer=i)
        ya, yb = _attention(qa, ka, va, qb, kb, vb, diff_lambda[i], diff_gain_lanes, lam_init=lam_init, latent=True)
        mixed = _mixer(ucd, decay_lanes, decay_heads, ret_gain_lanes, pool_w_bd, pool_scale[i][None, :],
                       with_ctx=not last)
        if last:
            ys, rows = [(ya,), (yb,), (mixed[0],), (mixed[1],)], T_LAT
        else:
            ya_c, yb_c = _attention(qa, ka, va, qb, kb, vb, diff_lambda[i], diff_gain_lanes, lam_init=lam_init,
                                    latent=False)
            ys, rows = [(ya, ya_c), (yb, yb_c), (mixed[0], mixed[2]), (mixed[1], mixed[3])], T_ALL
        h = ffn((h,), 2, rows, ys=ys, w_out=w_out_b)
        xs = (h,)
    return h.reshape(BATCH, SEQ, D_MODEL)
```

```python
import functools
import math

import jax
import jax.numpy as jnp
from jax import lax
from jax.experimental import pallas as pl
from jax.experimental.pallas import tpu as pltpu

F32 = jnp.float32
BF16 = jnp.bfloat16

D_MODEL = 1024
BATCH = 8
SEQ = 2048
DEPTH = 2
CTX_LEN = 256
GRID_W = 64
HEAD_DIM = 64
DIFF_QK_DIM = 32
GROUP_WIDTH = 256
D_FF = 2816
D_IN = 2560
FFN_RESIDUAL = 0.5
ROPE_BASE = 10000.0
EPS = 1e-6

T_LAT = BATCH * SEQ
T_CTX = BATCH * CTX_LEN
T_ALL = T_LAT + T_CTX
MOD_ROWS = 16

TM_FFN = 512
FFN_SPLIT = 2
TM_PROJ = 512
LOG2E = 1.4426950408889634
TQ = 256
RET_C = 256
VMEM_LIMIT = 56 * 1024 * 1024


def _cparams(sem):
    return pltpu.CompilerParams(dimension_semantics=sem, vmem_limit_bytes=VMEM_LIMIT)


def _silu(x):
    return x * jax.nn.sigmoid(x)


_MXU = dict(preferred_element_type=F32, precision=lax.Precision.DEFAULT)


def _dot(a, b):
    return jnp.dot(a, b, **_MXU)


def _dot_nt(a, b):
    return lax.dot_general(a, b, (((1,), (1,)), ((), ())), **_MXU)


def _dot_tn(a, b):
    return lax.dot_general(a, b, (((0,), (0,)), ((), ())), **_MXU)


def _block_ones(n, blk):
    r = lax.broadcasted_iota(jnp.int32, (n, n), 0) // blk
    c = lax.broadcasted_iota(jnp.int32, (n, n), 1) // blk
    return r == c


def _group_sums(sq, ones_bf16):
    hi = sq.astype(BF16)
    lo = (sq - hi.astype(F32)).astype(BF16)
    return _dot(hi, ones_bf16) + _dot(lo, ones_bf16)


def _rms_rows(x):
    return lax.rsqrt(jnp.mean(x * x, axis=-1, keepdims=True) + EPS)


def _mod_row_index(tile, tm):
    return jnp.minimum((tile * tm) // SEQ, BATCH)


def _mod_kernel(cc_ref, w_ref, b_ref, o_ref):
    o_ref[...] = _dot(_silu(cc_ref[...]), w_ref[...]) + b_ref[...]


def _modulation(cc, w_mod, b_mod):
    tn = 1024
    n_out = 9 * D_MODEL
    return pl.pallas_call(
        _mod_kernel,
        grid=(DEPTH, n_out // tn),
        in_specs=[
            pl.BlockSpec((MOD_ROWS, D_MODEL), lambda l, j: (0, 0)),
            pl.BlockSpec((None, D_MODEL, tn), lambda l, j: (l, 0, j)),
            pl.BlockSpec((None, 1, tn), lambda l, j: (l, 0, j)),
        ],
        out_specs=pl.BlockSpec((None, MOD_ROWS, tn), lambda l, j: (l, 0, j)),
        out_shape=jax.ShapeDtypeStruct((DEPTH, MOD_ROWS, n_out), F32),
        compiler_params=_cparams(("parallel", "parallel")),
        name="adaln_modulation",
    )(cc, w_mod, b_mod.reshape(DEPTH, 1, n_out))


def _token_tile(refs, rows, lat_tiles):
    if len(refs) == 1:
        return refs[0][rows, :]
    return jnp.where(pl.program_id(0) < lat_tiles, refs[0][rows, :], refs[1][rows, :])


def _ffn_kernel(*refs, n_x, n_y, sub, g_pre, g_post, lat_tiles):
    x_refs = refs[:n_x]
    m_ref, g_ref = refs[n_x:n_x + 2]
    y_refs = refs[n_x + 2:n_x + 2 + 4 * n_y]
    wo_ref = refs[n_x + 2 + 4 * n_y] if n_y else None
    wg_ref, wu_ref, wd_ref, o_ref = refs[-4:]
    pre = g_ref[g_pre:g_pre + 1, :] * (1.0 + m_ref[3 * sub + 1:3 * sub + 2, :])
    shift = m_ref[3 * sub:3 * sub + 1, :]
    post = (FFN_RESIDUAL * m_ref[3 * sub + 2:3 * sub + 3, :]) * g_ref[g_post:g_post + 1, :]
    mix_post = m_ref[5:6, :] * g_ref[3:4, :]
    rs = o_ref.shape[0] // FFN_SPLIT
    for s in range(FFN_SPLIT):
        rows = pl.ds(s * rs, rs)
        x = _token_tile(x_refs, rows, lat_tiles)
        if n_y:
            y = None
            for j in range(4):
                part = _dot(_token_tile(y_refs[n_y * j:n_y * (j + 1)], rows, lat_tiles).astype(F32),
                            wo_ref[256 * j:256 * (j + 1), :])
                y = part if y is None else y + part
            x = x + (y * _rms_rows(y)) * mix_post
        hm = x * _rms_rows(x) * pre + shift
        gate = _dot(hm, wg_ref[...])
        up = _dot(hm, wu_ref[...])
        y = _dot(_silu(gate) * up, wd_ref[...])
        o_ref[rows, :] = x + (y * _rms_rows(y)) * post


def _stream_specs(xs, tm, width):
    if len(xs) == 1:
        return [pl.BlockSpec((tm, width), lambda i: (i, 0))]
    lat_tiles = xs[0].shape[0] // tm
    return [pl.BlockSpec((tm, width), lambda i: (jnp.minimum(i, lat_tiles - 1), 0)),
            pl.BlockSpec((tm, width), lambda i: (jnp.maximum(i - lat_tiles, 0), 0))]


def _ffn_sublayer(xs, mod, gains, wg, wu, wd, *, layer, sub, rows, ys=(), w_out=None):
    tm = TM_FFN
    n_y = len(ys[0]) if ys else 0
    kern = functools.partial(_ffn_kernel, n_x=len(xs), n_y=n_y, sub=sub, g_pre=2 * sub, g_post=2 * sub + 1,
                             lat_tiles=T_LAT // tm)
    k = sub // 2
    resident = dict(pipeline_mode=pl.Buffered(1))
    y_specs, y_args = [], []
    for parts in ys:
        y_specs += _stream_specs(parts, tm, 256)
        y_args += list(parts)
    if ys:
        y_specs.append(pl.BlockSpec((None, D_MODEL, D_MODEL), lambda i: (layer, 0, 0), **resident))
        y_args.append(w_out)
    return pl.pallas_call(
        kern,
        grid=(rows // tm,),
        in_specs=_stream_specs(xs, tm, D_MODEL) + [
            pl.BlockSpec((None, 9, D_MODEL), lambda i: (_mod_row_index(i, tm), 0, 0)),
            pl.BlockSpec((6, D_MODEL), lambda i: (0, 0)),
        ] + y_specs + [
            pl.BlockSpec((None, None, D_MODEL, D_FF), lambda i: (layer, k, 0, 0), **resident),
            pl.BlockSpec((None, None, D_MODEL, D_FF), lambda i: (layer, k, 0, 0), **resident),
            pl.BlockSpec((None, None, D_FF, D_MODEL), lambda i: (layer, k, 0, 0), **resident),
        ],
        out_specs=pl.BlockSpec((tm, D_MODEL), lambda i: (i, 0)),
        out_shape=jax.ShapeDtypeStruct((rows, D_MODEL), F32),
        compiler_params=_cparams(("parallel",)),
        name=f"ffn_sublayer_{sub}",
    )(*xs, mod, gains, *y_args, wg, wu, wd)


def _rope(x, tab_ref, rows, quarter):
    w = x.shape[1]
    reps = w // tab_ref.shape[2]

    def tab(k):
        t = tab_ref[k, rows, :]
        return t if reps == 1 else jnp.concatenate([t] * reps, axis=1)

    return x * tab(0) + pltpu.roll(x, w - quarter, 1) * tab(1) + pltpu.roll(x, quarter, 1) * tab(2)


def _swap_middle_heads(x):
    first = lax.broadcasted_iota(jnp.int32, (1, 128), 1) < HEAD_DIM
    lo, hi = x[:, 0:128], x[:, 128:256]
    return jnp.concatenate([jnp.where(first, lo, pltpu.roll(hi, HEAD_DIM, 1)),
                            jnp.where(first, pltpu.roll(lo, HEAD_DIM, 1), hi)], axis=1)


def _inproj_kernel(x_ref, m_ref, g_ref, w_ref, ta_ref, tb_ref, qkg_ref,
                   qa_ref, ka_ref, va_ref, qb_ref, kb_ref, vb_ref, ucd_ref):
    pre = g_ref[2:3, :] * (1.0 + m_ref[4:5, :])
    shift = m_ref[3:4, :]
    ones = _block_ones(256, HEAD_DIM).astype(BF16)
    inv_d = 1.0 / HEAD_DIM
    lane128 = lax.broadcasted_iota(jnp.int32, (1, 128), 1)
    rs = x_ref.shape[0] // 2
    for s in range(2):
        rows = pl.ds(s * rs, rs)
        x = x_ref[rows, :]
        hm = x * _rms_rows(x) * pre + shift

        def proj(lo, hi):
            return _dot(hm, w_ref[:, lo:hi])

        q = proj(0, 256)
        q = q * lax.rsqrt(_group_sums(q * q, ones) * inv_d + EPS) * qkg_ref[0:1, :]
        q = _rope(q, ta_ref, rows, HEAD_DIM // 4) * (HEAD_DIM ** -0.5 * LOG2E)
        qa_ref[rows, :] = _swap_middle_heads(q).astype(BF16)
        k = proj(256, 384)
        k = k * lax.rsqrt(_group_sums(k * k, ones[:128, :128]) * inv_d + EPS) * qkg_ref[1:2, :128]
        ka_ref[rows, :] = _rope(k, ta_ref, rows, HEAD_DIM // 4).astype(BF16)
        v = proj(384, 512)
        for g in range(2):
            va_ref[rows, 128 * g:128 * g + 128] = jnp.where(lane128 == HEAD_DIM * (1 - g), 1.0, v).astype(BF16)

        qb_ref[rows, :] = (_rope(proj(512, 768), tb_ref, rows, DIFF_QK_DIM // 4)
                           * (DIFF_QK_DIM ** -0.5 * LOG2E)).astype(BF16)
        kb_ref[rows, :] = _rope(proj(768, 1024), tb_ref, rows, DIFF_QK_DIM // 4).astype(BF16)
        v = proj(1024, 1280)
        for hd in range(4):
            own = (lane128 // HEAD_DIM) == hd % 2
            ones_col = jnp.where(lane128 == HEAD_DIM * (1 - hd % 2), 1.0, 0.0)
            pair = v[:, 128 * (hd // 2):128 * (hd // 2) + 128]
            vb_ref[rows, 128 * hd:128 * hd + 128] = jnp.where(own, pair, ones_col).astype(BF16)

        ucd_ref[rows, :] = proj(1280, 2560)


def _in_projection(h, mod, gains, w_in, tab_a, tab_b, qk_gain_lanes, *, layer):
    tm = TM_PROJ
    lat_tiles = T_LAT // tm
    tiles_per_seq = SEQ // tm

    def tab_idx(i):
        return jnp.where(i < lat_tiles, i % tiles_per_seq, tiles_per_seq)

    def row_spec(width):
        return pl.BlockSpec((tm, width), lambda i: (i, 0))

    def out(width, dtype):
        return jax.ShapeDtypeStruct((T_ALL, width), dtype)

    return pl.pallas_call(
        _inproj_kernel,
        grid=(T_ALL // tm,),
        in_specs=[
            row_spec(D_MODEL),
            pl.BlockSpec((None, 9, D_MODEL), lambda i: (_mod_row_index(i, tm), 0, 0)),
            pl.BlockSpec((6, D_MODEL), lambda i: (0, 0)),
            pl.BlockSpec((None, D_MODEL, D_IN), lambda i: (layer, 0, 0), pipeline_mode=pl.Buffered(1)),
            pl.BlockSpec((3, tm, 128), lambda i: (0, tab_idx(i), 0)),
            pl.BlockSpec((3, tm, 128), lambda i: (0, tab_idx(i), 0)),
            pl.BlockSpec((2, 256), lambda i: (0, 0)),
        ],
        out_specs=[row_spec(256), row_spec(128), row_spec(256), row_spec(256), row_spec(256), row_spec(512),
                   row_spec(1280)],
        out_shape=[out(256, BF16), out(128, BF16), out(256, BF16), out(256, BF16), out(256, BF16), out(512, BF16),
                   out(1280, F32)],
        compiler_params=_cparams(("parallel",)),
        name="in_projection",
    )(h, mod, gains, w_in, tab_a, tab_b, qk_gain_lanes)


def _softmax_pv(qs, k_refs, v_refs, lanes, ones_lane):
    ss = [_dot_nt(qs, k[...]) for k in k_refs]
    m = ss[0].max(axis=-1, keepdims=True)
    for s in ss[1:]:
        m = jnp.maximum(m, s.max(axis=-1, keepdims=True))
    ps = [jnp.exp2(s - m).astype(BF16) for s in ss]
    half = qs.shape[0] // 2
    outs = []
    for r0 in (0, half):
        o = None
        for p, v in zip(ps, v_refs):
            pv = _dot(p[r0:r0 + half], v[:, lanes])
            o = pv if o is None else o + pv
        outs.append(o / o[:, ones_lane:ones_lane + 1])
    return outs


def _attn_kernel(*refs, n_parts, lam_init):
    qa_ref = refs[0]
    ka = refs[1:1 + n_parts]
    va = refs[1 + n_parts:1 + 2 * n_parts]
    qb_ref = refs[1 + 2 * n_parts]
    kb = refs[2 + 2 * n_parts:2 + 3 * n_parts]
    vb = refs[2 + 3 * n_parts:2 + 4 * n_parts]
    lam_ref, dg_ref, ya_ref, yb_ref = refs[2 + 4 * n_parts:]

    lane128 = lax.broadcasted_iota(jnp.int32, (1, 128), 1)
    first = lane128 < HEAD_DIM
    qa = qa_ref[...]
    zero = jnp.zeros((), BF16)
    outs = []
    for g in range(2):
        sel = (lane128 // HEAD_DIM) == g
        qs = jnp.concatenate([jnp.where(sel, qa[:, 0:128], zero), jnp.where(sel, qa[:, 128:256], zero)], axis=0)
        outs.append(_softmax_pv(qs, ka, va, slice(128 * g, 128 * g + 128), HEAD_DIM * (1 - g)))
    y0 = jnp.where(first, outs[0][0], outs[1][0])
    y1 = jnp.where(first, outs[0][1], outs[1][1])
    ya_ref[...] = _swap_middle_heads(jnp.concatenate([y0, y1], axis=1)).astype(BF16)

    lp = lam_ref[...]
    lam = (jnp.exp(jnp.sum(lp[0:1] * lp[1:2], axis=-1, keepdims=True))
           - jnp.exp(jnp.sum(lp[2:3] * lp[3:4], axis=-1, keepdims=True)) + lam_init)
    half = lax.broadcasted_iota(jnp.int32, (1, 256), 1) // DIFF_QK_DIM
    qb = qb_ref[...]
    d = []
    for h in range(4):
        qs = jnp.concatenate([jnp.where(half == 2 * h, qb, zero), jnp.where(half == 2 * h + 1, qb, zero)], axis=0)
        o1, o2 = _softmax_pv(qs, kb, vb, slice(128 * h, 128 * h + 128), HEAD_DIM * (1 - h % 2))
        d.append(o1 - lam * o2)
    y = jnp.concatenate([jnp.where(first, d[0], d[1]), jnp.where(first, d[2], d[3])], axis=1)
    ones = _block_ones(256, HEAD_DIM).astype(BF16)
    yn = y * lax.rsqrt(_group_sums(y * y, ones) * (1.0 / HEAD_DIM) + EPS) * dg_ref[...]
    yb_ref[...] = (yn * (1.0 - lam_init)).astype(BF16)


def _attention(qa, ka, va, qb, kb, vb, diff_lambda, diff_gain_lanes, *, lam_init, latent):
    ctx_blk0 = T_LAT // CTX_LEN
    if latent:
        grid = (BATCH, SEQ // TQ)
        q_map = lambda b, i: (b * (SEQ // TQ) + i, 0)
        kv_specs = lambda w: [pl.BlockSpec((CTX_LEN, w), lambda b, i: (ctx_blk0 + b, 0)),
                              pl.BlockSpec((SEQ, w), lambda b, i: (b, 0))]
        rows = T_LAT
        n_parts = 2
        const = lambda b, i: (0, 0)
        o_map = q_map
    else:
        grid = (BATCH,)
        q_map = lambda b: (ctx_blk0 + b, 0)
        kv_specs = lambda w: [pl.BlockSpec((CTX_LEN, w), lambda b: (ctx_blk0 + b, 0))]
        rows = T_CTX
        n_parts = 1
        const = lambda b: (0, 0)
        o_map = lambda b: (b, 0)
    in_specs = ([pl.BlockSpec((TQ, 256), q_map)] + kv_specs(128) + kv_specs(256)
                + [pl.BlockSpec((TQ, 256), q_map)] + kv_specs(256) + kv_specs(512)
                + [pl.BlockSpec((4, DIFF_QK_DIM), const), pl.BlockSpec((1, 256), const)])
    args = [qa] + [ka] * n_parts + [va] * n_parts + [qb] + [kb] * n_parts + [vb] * n_parts
    return pl.pallas_call(
        functools.partial(_attn_kernel, n_parts=n_parts, lam_init=lam_init),
        grid=grid,
        in_specs=in_specs,
        out_specs=[pl.BlockSpec((TQ, 256), o_map), pl.BlockSpec((TQ, 256), o_map)],
        out_shape=[jax.ShapeDtypeStruct((rows, 256), BF16), jax.ShapeDtypeStruct((rows, 256), BF16)],
        compiler_params=_cparams(("parallel",) * len(grid)),
        name="attention_latent" if latent else "attention_context",
    )(*args, diff_lambda, diff_gain_lanes)


def _pool(u, pw, ps):
    n_seq = u.shape[0]
    pad = 16
    n = n_seq + 2 * pad
    z = jnp.zeros((pad, 256), F32)
    p1 = jnp.concatenate([z, u, z], axis=0)
    s2 = p1 + pltpu.roll(p1, n - 1, 0)
    s4 = s2 + pltpu.roll(s2, n - 2, 0)
    s8 = s4 + pltpu.roll(s4, n - 4, 0)
    s16 = s8 + pltpu.roll(s8, n - 8, 0)
    w2 = pltpu.roll(s2, 1, 0)[pad:pad + n_seq]
    w4 = pltpu.roll(s4, 2, 0)[pad:pad + n_seq]
    w8 = pltpu.roll(s8, 4, 0)[pad:pad + n_seq]
    w16 = pltpu.roll(s16, 8, 0)[pad:pad + n_seq]
    grp = lax.broadcasted_iota(jnp.int32, (1, 256), 1) // 64
    wsum = jnp.where(grp == 0, w2, jnp.where(grp == 1, w4, jnp.where(grp == 2, w8, w16)))
    back = jnp.where(grp == 0, 1, jnp.where(grp == 1, 2, jnp.where(grp == 2, 4, 8)))
    t = lax.broadcasted_iota(jnp.int32, (n_seq, 256), 0)
    count = jnp.minimum(t + (back - 1), n_seq - 1) - jnp.maximum(t - back, 0) + 1
    pooled = wsum / count.astype(F32) - u
    return _dot(pooled, pw) * ps


def _log_sigmoid(x):
    return jnp.minimum(x, 0.0) - jnp.log(1.0 + jnp.exp(-jnp.abs(x)))


def _mix_kernel(*refs, with_ctx):
    if with_ctx:
        (ul_ref, uc_ref, dl_ref, dh_ref, rg_ref, pw_ref, ps_ref,
         ycl_ref, ydl_ref, ycc_ref, ydc_ref, tst_ref) = refs
    else:
        (ul_ref, uc_ref, dl_ref, dh_ref, rg_ref, pw_ref, ps_ref, ycl_ref, ydl_ref, tst_ref) = refs
    c = RET_C
    n_chunks = SEQ // c
    pw = pw_ref[...]
    ps = ps_ref[...]

    ycl_ref[...] = _pool(ul_ref[:, 0:256], pw, ps).astype(BF16)
    if with_ctx:
        ycc_ref[...] = _pool(uc_ref[:, 0:256], pw, ps).astype(BF16)

    lg = _log_sigmoid(dl_ref[...])
    lgh = _log_sigmoid(dh_ref[...])
    lgf, lgb = lg[0:1, :], lg[1:2, :]
    pos = lax.broadcasted_iota(jnp.int32, (c, 1), 0).astype(F32)
    wkf = jnp.exp((c - 1.0 - pos) * lgf)
    wkb = jnp.exp(pos * lgb)
    wqf = jnp.exp((pos + 1.0) * lgf)
    wqb = jnp.exp((c - pos) * lgb)
    dcf = jnp.exp(c * lgf)
    dcb = jnp.exp(c * lgb)
    same_head = _block_ones(256, HEAD_DIM)
    ones = same_head.astype(BF16)
    lane = lax.broadcasted_iota(jnp.int32, (1, 256), 1) // HEAD_DIM
    diff = (lax.broadcasted_iota(jnp.int32, (c, c), 0) - lax.broadcasted_iota(jnp.int32, (c, c), 1)).astype(F32)
    decay = []
    for h in range(4):
        df = jnp.exp(jnp.maximum(diff, 0.0) * lgh[h:h + 1, 0:1])
        db = jnp.exp(jnp.maximum(-diff, 0.0) * lgh[4 + h:5 + h, 0:1])
        decay.append(jnp.where(diff > 0, df, jnp.where(diff < 0, db, 2.0)))
    rg = rg_ref[...]
    k_scale = HEAD_DIM ** -0.5

    def load(ref, r0):
        rows = pl.ds(r0, c)
        return ref[rows, 256:512], ref[rows, 512:768] * k_scale, ref[rows, 768:1024], ref[rows, 1024:1280]

    def increment(k, v, wk):
        return jnp.where(same_head, _dot_tn(k * wk, v), 0.0)

    def chunk_out(q, k, v, gate, s_fwd, s_bwd):
        o = _dot(q * wqf, s_fwd) + _dot(q * wqb, s_bwd)
        for h in range(4):
            sel = lane == h
            s = _dot_nt(jnp.where(sel, q, 0.0), k)
            o = o + _dot(s * decay[h], jnp.where(sel, v, 0.0))
        on = o * lax.rsqrt(_group_sums(o * o, ones) * (1.0 / HEAD_DIM) + EPS) * rg
        return (on * _silu(gate)).astype(BF16)

    _, kc, vc, _ = load(uc_ref, 0)
    s_fwd = increment(kc, vc, wkf)
    s_bwd = increment(kc, vc, wkb)

    for j in range(n_chunks - 1, -1, -1):
        tst_ref[j] = s_bwd
        if j > 0:
            _, k, v, _ = load(ul_ref, j * c)
            s_bwd = s_bwd * dcb + increment(k, v, wkb)

    for j in range(n_chunks):
        q, k, v, gate = load(ul_ref, j * c)
        ydl_ref[pl.ds(j * c, c), :] = chunk_out(q, k, v, gate, s_fwd, tst_ref[j])
        if j < n_chunks - 1:
            s_fwd = s_fwd * dcf + increment(k, v, wkf)

    if with_ctx:
        qc, kc, vc, gc = load(uc_ref, 0)
        zeros = jnp.zeros((256, 256), F32)
        ydc_ref[...] = chunk_out(qc, kc, vc, gc, zeros, zeros)


def _mixer(ucd, decay_lanes, decay_heads, ret_gain_lanes, pool_w_bd, pool_scale, *, with_ctx):
    ctx_blk0 = T_LAT // CTX_LEN
    const = lambda b: (0, 0)
    out_specs = [pl.BlockSpec((SEQ, 256), lambda b: (b, 0))] * 2
    out_shape = [jax.ShapeDtypeStruct((T_LAT, 256), BF16)] * 2
    if with_ctx:
        out_specs = out_specs + [pl.BlockSpec((CTX_LEN, 256), lambda b: (b, 0))] * 2
        out_shape = out_shape + [jax.ShapeDtypeStruct((T_CTX, 256), BF16)] * 2
    return pl.pallas_call(
        functools.partial(_mix_kernel, with_ctx=with_ctx),
        grid=(BATCH,),
        in_specs=[
            pl.BlockSpec((SEQ, 1280), lambda b: (b, 0)),
            pl.BlockSpec((CTX_LEN, 1280), lambda b: (ctx_blk0 + b, 0)),
            pl.BlockSpec((2, 256), const),
            pl.BlockSpec((8, 128), const),
            pl.BlockSpec((1, 256), const),
            pl.BlockSpec((256, 256), const),
            pl.BlockSpec((1, 256), const),
        ],
        out_specs=out_specs,
        out_shape=out_shape,
        scratch_shapes=[pltpu.VMEM((SEQ // RET_C, 256, 256), F32)],
        compiler_params=_cparams(("parallel",)),
        name="pool_retention_mixer",
    )(ucd, ucd, decay_lanes, decay_heads, ret_gain_lanes, pool_w_bd, pool_scale)


def _rope_tables(dim, tm):
    q = dim // 4
    n_rows = SEQ // GRID_W
    rows = jnp.repeat(jnp.arange(n_rows), GRID_W).astype(F32)
    cols = jnp.tile(jnp.arange(GRID_W), n_rows).astype(F32)
    inv = ROPE_BASE ** (-jnp.arange(q, dtype=F32) / q)
    ar, ac = rows[:, None] * inv, cols[:, None] * inv
    cos = jnp.concatenate([jnp.cos(ar)] * 2 + [jnp.cos(ac)] * 2, axis=-1)
    sin = jnp.concatenate([jnp.sin(ar)] * 2 + [jnp.sin(ac)] * 2, axis=-1)
    first_half = jnp.tile(jnp.repeat(jnp.array([True, False]), q), 2)
    tabs = jnp.stack([cos, jnp.where(first_half, -sin, 0.0), jnp.where(first_half, 0.0, sin)])
    tabs = jnp.tile(tabs, (1, 1, 128 // dim))
    ident = jnp.stack([jnp.ones((tm, 128), F32), jnp.zeros((tm, 128), F32), jnp.zeros((tm, 128), F32)])
    return jnp.concatenate([tabs, ident], axis=1)


def _block_diag(blocks):
    n, d, _ = blocks.shape
    eye = jnp.eye(n, dtype=blocks.dtype)
    return jnp.einsum('gcd,gh->gchd', blocks, eye).reshape(n * d, n * d)


def kernel(x, c, ctx, c_ctx, w_mod, b_mod, norm_gain, ffn_w_gate, ffn_w_up, ffn_w_down, w_in, w_out,
           attn_qk_gain, diff_lambda, diff_out_gain, pool_w, pool_scale, ret_decay_logit, ret_out_gain):
    xs = (x.reshape(T_LAT, D_MODEL), ctx.reshape(T_CTX, D_MODEL))
    cc = jnp.concatenate([c, c_ctx[None, :], jnp.zeros((MOD_ROWS - BATCH - 1, D_MODEL), F32)], axis=0)
    mod_all = _modulation(cc, w_mod, b_mod).reshape(DEPTH, MOD_ROWS, 9, D_MODEL)
    tab_a = _rope_tables(HEAD_DIM, TM_PROJ)
    tab_b = _rope_tables(DIFF_QK_DIM, TM_PROJ)

    for i in range(DEPTH):
        last = i == DEPTH - 1
        mod = mod_all[i]
        gains = norm_gain[i]
        qk_gain_lanes = jnp.tile(attn_qk_gain[i], (1, 256 // HEAD_DIM))
        diff_gain_lanes = jnp.tile(diff_out_gain[i], 256 // HEAD_DIM)[None, :]
        ret_gain_lanes = jnp.tile(ret_out_gain[i], 256 // HEAD_DIM)[None, :]
        decay_lanes = jnp.repeat(ret_decay_logit[i], HEAD_DIM, axis=1)
        decay_heads = jnp.broadcast_to(ret_decay_logit[i].reshape(8, 1), (8, 128))
        pool_w_bd = _block_diag(pool_w[i])
        lam_init = 0.8 - 0.6 * math.exp(-0.3 * i)

        def ffn(hs, sub, rows, **mix):
            return _ffn_sublayer(hs, mod, gains, ffn_w_gate, ffn_w_up, ffn_w_down, layer=i, sub=sub, rows=rows,
                                 **mix)

        h = ffn(xs, 0, T_ALL)
        qa, ka, va, qb, kb, vb, ucd = _in_projection(h, mod, gains, w_in, tab_a, tab_b, qk_gain_lanes, layer=i)
        ya, yb = _attention(qa, ka, va, qb, kb, vb, diff_lambda[i], diff_gain_lanes, lam_init=lam_init, latent=True)
        mixed = _mixer(ucd, decay_lanes, decay_heads, ret_gain_lanes, pool_w_bd, pool_scale[i][None, :],
                       with_ctx=not last)
        if last:
            ys, rows = [(ya,), (yb,), (mixed[0],), (mixed[1],)], T_LAT
        else:
            ya_c, yb_c = _attention(qa, ka, va, qb, kb, vb, diff_lambda[i], diff_gain_lanes, lam_init=lam_init,
                                    latent=False)
            ys, rows = [(ya, ya_c), (yb, yb_c), (mixed[0], mixed[2]), (mixed[1], mixed[3])], T_ALL
        h = ffn((h,), 2, rows, ys=ys, w_out=w_out)
        xs = (h,)
    return h.reshape(BATCH, SEQ, D_MODEL)
```

```python
import functools
import math

import jax
import jax.numpy as jnp
from jax import lax
from jax.experimental import pallas as pl
from jax.experimental.pallas import tpu as pltpu

F32 = jnp.float32
BF16 = jnp.bfloat16

D_MODEL = 1024
BATCH = 8
SEQ = 2048
DEPTH = 2
CTX_LEN = 256
GRID_W = 64
HEAD_DIM = 64
DIFF_QK_DIM = 32
GROUP_WIDTH = 256
D_FF = 2816
D_IN = 2560
FFN_RESIDUAL = 0.5
ROPE_BASE = 10000.0
EPS = 1e-6

T_LAT = BATCH * SEQ
T_CTX = BATCH * CTX_LEN
T_ALL = T_LAT + T_CTX
MOD_ROWS = 16

TM_FFN = 512
FFN_SPLIT = 2
TM_PROJ = 1024
PROJ_SPLIT = 4
LOG2E = 1.4426950408889634
TQ = 256
RET_C = 256
VMEM_LIMIT = 56 * 1024 * 1024


def _cparams(sem):
    return pltpu.CompilerParams(dimension_semantics=sem, vmem_limit_bytes=VMEM_LIMIT)


def _silu(x):
    return x * jax.nn.sigmoid(x)


_MXU = dict(preferred_element_type=F32, precision=lax.Precision.DEFAULT)


def _dot(a, b):
    return jnp.dot(a, b, **_MXU)


def _dot_nt(a, b):
    return lax.dot_general(a, b, (((1,), (1,)), ((), ())), **_MXU)


def _dot_tn(a, b):
    return lax.dot_general(a, b, (((0,), (0,)), ((), ())), **_MXU)


def _block_ones(n, blk):
    r = lax.broadcasted_iota(jnp.int32, (n, n), 0) // blk
    c = lax.broadcasted_iota(jnp.int32, (n, n), 1) // blk
    return r == c


def _group_sums(sq, ones_bf16):
    hi = sq.astype(BF16)
    lo = (sq - hi.astype(F32)).astype(BF16)
    return _dot(hi, ones_bf16) + _dot(lo, ones_bf16)


def _rms_rows(x):
    return lax.rsqrt(jnp.mean(x * x, axis=-1, keepdims=True) + EPS)


def _mod_row_index(tile, tm):
    return jnp.minimum((tile * tm) // SEQ, BATCH)


def _mod_kernel(cc_ref, w_ref, b_ref, o_ref):
    o_ref[...] = _dot(_silu(cc_ref[...]), w_ref[...]) + b_ref[...]


def _modulation(cc, w_mod, b_mod):
    tn = 1024
    n_out = 9 * D_MODEL
    return pl.pallas_call(
        _mod_kernel,
        grid=(DEPTH, n_out // tn),
        in_specs=[
            pl.BlockSpec((MOD_ROWS, D_MODEL), lambda l, j: (0, 0)),
            pl.BlockSpec((None, D_MODEL, tn), lambda l, j: (l, 0, j)),
            pl.BlockSpec((None, 1, tn), lambda l, j: (l, 0, j)),
        ],
        out_specs=pl.BlockSpec((None, MOD_ROWS, tn), lambda l, j: (l, 0, j)),
        out_shape=jax.ShapeDtypeStruct((DEPTH, MOD_ROWS, n_out), F32),
        compiler_params=_cparams(("parallel", "parallel")),
        name="adaln_modulation",
    )(cc, w_mod, b_mod.reshape(DEPTH, 1, n_out))


def _token_tile(refs, rows, lat_tiles):
    if len(refs) == 1:
        return refs[0][rows, :]
    return jnp.where(pl.program_id(0) < lat_tiles, refs[0][rows, :], refs[1][rows, :])


def _ffn_kernel(*refs, n_x, n_y, sub, g_pre, g_post, lat_tiles):
    x_refs = refs[:n_x]
    m_ref, g_ref = refs[n_x:n_x + 2]
    y_refs = refs[n_x + 2:n_x + 2 + 4 * n_y]
    wo_ref = refs[n_x + 2 + 4 * n_y] if n_y else None
    wg_ref, wu_ref, wd_ref, o_ref = refs[-4:]
    pre = g_ref[g_pre:g_pre + 1, :] * (1.0 + m_ref[3 * sub + 1:3 * sub + 2, :])
    shift = m_ref[3 * sub:3 * sub + 1, :]
    post = (FFN_RESIDUAL * m_ref[3 * sub + 2:3 * sub + 3, :]) * g_ref[g_post:g_post + 1, :]
    mix_post = m_ref[5:6, :] * g_ref[3:4, :]
    rs = o_ref.shape[0] // FFN_SPLIT
    for s in range(FFN_SPLIT):
        rows = pl.ds(s * rs, rs)
        x = _token_tile(x_refs, rows, lat_tiles)
        if n_y:
            y = None
            for j in range(4):
                part = _dot(_token_tile(y_refs[n_y * j:n_y * (j + 1)], rows, lat_tiles).astype(F32),
                            wo_ref[256 * j:256 * (j + 1), :])
                y = part if y is None else y + part
            x = x + (y * _rms_rows(y)) * mix_post
        hm = x * _rms_rows(x) * pre + shift
        gate = _dot(hm, wg_ref[...])
        up = _dot(hm, wu_ref[...])
        y = _dot(_silu(gate) * up, wd_ref[...])
        o_ref[rows, :] = x + (y * _rms_rows(y)) * post


def _stream_specs(xs, tm, width):
    if len(xs) == 1:
        return [pl.BlockSpec((tm, width), lambda i: (i, 0))]
    lat_tiles = xs[0].shape[0] // tm
    return [pl.BlockSpec((tm, width), lambda i: (jnp.minimum(i, lat_tiles - 1), 0)),
            pl.BlockSpec((tm, width), lambda i: (jnp.maximum(i - lat_tiles, 0), 0))]


def _ffn_sublayer(xs, mod, gains, wg, wu, wd, *, layer, sub, rows, ys=(), w_out=None):
    tm = TM_FFN
    n_y = len(ys[0]) if ys else 0
    kern = functools.partial(_ffn_kernel, n_x=len(xs), n_y=n_y, sub=sub, g_pre=2 * sub, g_post=2 * sub + 1,
                             lat_tiles=T_LAT // tm)
    k = sub // 2
    resident = dict(pipeline_mode=pl.Buffered(1))
    y_specs, y_args = [], []
    for parts in ys:
        y_specs += _stream_specs(parts, tm, 256)
        y_args += list(parts)
    if ys:
        y_specs.append(pl.BlockSpec((None, D_MODEL, D_MODEL), lambda i: (layer, 0, 0), **resident))
        y_args.append(w_out)
    return pl.pallas_call(
        kern,
        grid=(rows // tm,),
        in_specs=_stream_specs(xs, tm, D_MODEL) + [
            pl.BlockSpec((None, 9, D_MODEL), lambda i: (_mod_row_index(i, tm), 0, 0)),
            pl.BlockSpec((6, D_MODEL), lambda i: (0, 0)),
        ] + y_specs + [
            pl.BlockSpec((None, None, D_MODEL, D_FF), lambda i: (layer, k, 0, 0), **resident),
            pl.BlockSpec((None, None, D_MODEL, D_FF), lambda i: (layer, k, 0, 0), **resident),
            pl.BlockSpec((None, None, D_FF, D_MODEL), lambda i: (layer, k, 0, 0), **resident),
        ],
        out_specs=pl.BlockSpec((tm, D_MODEL), lambda i: (i, 0)),
        out_shape=jax.ShapeDtypeStruct((rows, D_MODEL), F32),
        compiler_params=_cparams(("parallel",)),
        name=f"ffn_sublayer_{sub}",
    )(*xs, mod, gains, *y_args, wg, wu, wd)


def _rope(x, tab_ref, rows, quarter):
    w = x.shape[1]
    reps = w // tab_ref.shape[2]

    def tab(k):
        t = tab_ref[k, rows, :]
        return t if reps == 1 else jnp.concatenate([t] * reps, axis=1)

    return x * tab(0) + pltpu.roll(x, w - quarter, 1) * tab(1) + pltpu.roll(x, quarter, 1) * tab(2)


def _swap_middle_heads(x):
    first = lax.broadcasted_iota(jnp.int32, (1, 128), 1) < HEAD_DIM
    lo, hi = x[:, 0:128], x[:, 128:256]
    return jnp.concatenate([jnp.where(first, lo, pltpu.roll(hi, HEAD_DIM, 1)),
                            jnp.where(first, pltpu.roll(lo, HEAD_DIM, 1), hi)], axis=1)


def _inproj_kernel(x_ref, m_ref, g_ref, w_ref, ta_ref, tb_ref, qkg_ref,
                   qa_ref, ka_ref, va_ref, qb_ref, kb_ref, vb_ref, ucd_ref):
    pre = g_ref[2:3, :] * (1.0 + m_ref[4:5, :])
    shift = m_ref[3:4, :]
    ones = _block_ones(256, HEAD_DIM).astype(BF16)
    inv_d = 1.0 / HEAD_DIM
    lane128 = lax.broadcasted_iota(jnp.int32, (1, 128), 1)
    rs = x_ref.shape[0] // PROJ_SPLIT
    for s in range(PROJ_SPLIT):
        rows = pl.ds(s * rs, rs)
        x = x_ref[rows, :]
        hm = x * _rms_rows(x) * pre + shift

        def proj(lo, hi):
            return _dot(hm, w_ref[:, lo:hi])

        q = proj(0, 256)
        q = q * lax.rsqrt(_group_sums(q * q, ones) * inv_d + EPS) * qkg_ref[0:1, :]
        q = _rope(q, ta_ref, rows, HEAD_DIM // 4) * (HEAD_DIM ** -0.5 * LOG2E)
        qa_ref[rows, :] = _swap_middle_heads(q).astype(BF16)
        k = proj(256, 384)
        k = k * lax.rsqrt(_group_sums(k * k, ones[:128, :128]) * inv_d + EPS) * qkg_ref[1:2, :128]
        ka_ref[rows, :] = _rope(k, ta_ref, rows, HEAD_DIM // 4).astype(BF16)
        v = proj(384, 512)
        for g in range(2):
            va_ref[rows, 128 * g:128 * g + 128] = jnp.where(lane128 == HEAD_DIM * (1 - g), 1.0, v).astype(BF16)

        qb_ref[rows, :] = (_rope(proj(512, 768), tb_ref, rows, DIFF_QK_DIM // 4)
                           * (DIFF_QK_DIM ** -0.5 * LOG2E)).astype(BF16)
        kb_ref[rows, :] = _rope(proj(768, 1024), tb_ref, rows, DIFF_QK_DIM // 4).astype(BF16)
        v = proj(1024, 1280)
        for hd in range(4):
            own = (lane128 // HEAD_DIM) == hd % 2
            ones_col = jnp.where(lane128 == HEAD_DIM * (1 - hd % 2), 1.0, 0.0)
            pair = v[:, 128 * (hd // 2):128 * (hd // 2) + 128]
            vb_ref[rows, 128 * hd:128 * hd + 128] = jnp.where(own, pair, ones_col).astype(BF16)

        ucd_ref[rows, :] = proj(1280, 2560)


def _in_projection(h, mod, gains, w_in, tab_a, tab_b, qk_gain_lanes, *, layer):
    tm = TM_PROJ
    lat_tiles = T_LAT // tm
    tiles_per_seq = SEQ // tm

    def tab_idx(i):
        return jnp.where(i < lat_tiles, i % tiles_per_seq, tiles_per_seq)

    def row_spec(width):
        return pl.BlockSpec((tm, width), lambda i: (i, 0))

    def out(width, dtype):
        return jax.ShapeDtypeStruct((T_ALL, width), dtype)

    return pl.pallas_call(
        _inproj_kernel,
        grid=(T_ALL // tm,),
        in_specs=[
            row_spec(D_MODEL),
            pl.BlockSpec((None, 9, D_MODEL), lambda i: (_mod_row_index(i, tm), 0, 0)),
            pl.BlockSpec((6, D_MODEL), lambda i: (0, 0)),
            pl.BlockSpec((None, D_MODEL, D_IN), lambda i: (layer, 0, 0), pipeline_mode=pl.Buffered(1)),
            pl.BlockSpec((3, tm, 128), lambda i: (0, tab_idx(i), 0)),
            pl.BlockSpec((3, tm, 128), lambda i: (0, tab_idx(i), 0)),
            pl.BlockSpec((2, 256), lambda i: (0, 0)),
        ],
        out_specs=[row_spec(256), row_spec(128), row_spec(256), row_spec(256), row_spec(256), row_spec(512),
                   row_spec(1280)],
        out_shape=[out(256, BF16), out(128, BF16), out(256, BF16), out(256, BF16), out(256, BF16), out(512, BF16),
                   out(1280, F32)],
        compiler_params=_cparams(("parallel",)),
        name="in_projection",
    )(h, mod, gains, w_in, tab_a, tab_b, qk_gain_lanes)


def _softmax_pv(qs, k_refs, v_refs, lanes, ones_lane):
    ss = [_dot_nt(qs, k[...]) for k in k_refs]
    m = ss[0].max(axis=-1, keepdims=True)
    for s in ss[1:]:
        m = jnp.maximum(m, s.max(axis=-1, keepdims=True))
    ps = [jnp.exp2(s - m).astype(BF16) for s in ss]
    half = qs.shape[0] // 2
    outs = []
    for r0 in (0, half):
        o = None
        for p, v in zip(ps, v_refs):
            pv = _dot(p[r0:r0 + half], v[:, lanes])
            o = pv if o is None else o + pv
        outs.append(o / o[:, ones_lane:ones_lane + 1])
    return outs


def _attn_kernel(*refs, n_parts, lam_init):
    qa_ref = refs[0]
    ka = refs[1:1 + n_parts]
    va = refs[1 + n_parts:1 + 2 * n_parts]
    qb_ref = refs[1 + 2 * n_parts]
    kb = refs[2 + 2 * n_parts:2 + 3 * n_parts]
    vb = refs[2 + 3 * n_parts:2 + 4 * n_parts]
    lam_ref, dg_ref, ya_ref, yb_ref = refs[2 + 4 * n_parts:]

    lane128 = lax.broadcasted_iota(jnp.int32, (1, 128), 1)
    first = lane128 < HEAD_DIM
    qa = qa_ref[...]
    zero = jnp.zeros((), BF16)
    outs = []
    for g in range(2):
        sel = (lane128 // HEAD_DIM) == g
        qs = jnp.concatenate([jnp.where(sel, qa[:, 0:128], zero), jnp.where(sel, qa[:, 128:256], zero)], axis=0)
        outs.append(_softmax_pv(qs, ka, va, slice(128 * g, 128 * g + 128), HEAD_DIM * (1 - g)))
    y0 = jnp.where(first, outs[0][0], outs[1][0])
    y1 = jnp.where(first, outs[0][1], outs[1][1])
    ya_ref[...] = _swap_middle_heads(jnp.concatenate([y0, y1], axis=1)).astype(BF16)

    lp = lam_ref[...]
    lam = (jnp.exp(jnp.sum(lp[0:1] * lp[1:2], axis=-1, keepdims=True))
           - jnp.exp(jnp.sum(lp[2:3] * lp[3:4], axis=-1, keepdims=True)) + lam_init)
    half = lax.broadcasted_iota(jnp.int32, (1, 256), 1) // DIFF_QK_DIM
    qb = qb_ref[...]
    d = []
    for h in range(4):
        qs = jnp.concatenate([jnp.where(half == 2 * h, qb, zero), jnp.where(half == 2 * h + 1, qb, zero)], axis=0)
        o1, o2 = _softmax_pv(qs, kb, vb, slice(128 * h, 128 * h + 128), HEAD_DIM * (1 - h % 2))
        d.append(o1 - lam * o2)
    y = jnp.concatenate([jnp.where(first, d[0], d[1]), jnp.where(first, d[2], d[3])], axis=1)
    ones = _block_ones(256, HEAD_DIM).astype(BF16)
    yn = y * lax.rsqrt(_group_sums(y * y, ones) * (1.0 / HEAD_DIM) + EPS) * dg_ref[...]
    yb_ref[...] = (yn * (1.0 - lam_init)).astype(BF16)


def _attention(qa, ka, va, qb, kb, vb, diff_lambda, diff_gain_lanes, *, lam_init, latent):
    ctx_blk0 = T_LAT // CTX_LEN
    if latent:
        grid = (BATCH, SEQ // TQ)
        q_map = lambda b, i: (b * (SEQ // TQ) + i, 0)
        kv_specs = lambda w: [pl.BlockSpec((CTX_LEN, w), lambda b, i: (ctx_blk0 + b, 0)),
                              pl.BlockSpec((SEQ, w), lambda b, i: (b, 0))]
        rows = T_LAT
        n_parts = 2
        const = lambda b, i: (0, 0)
        o_map = q_map
    else:
        grid = (BATCH,)
        q_map = lambda b: (ctx_blk0 + b, 0)
        kv_specs = lambda w: [pl.BlockSpec((CTX_LEN, w), lambda b: (ctx_blk0 + b, 0))]
        rows = T_CTX
        n_parts = 1
        const = lambda b: (0, 0)
        o_map = lambda b: (b, 0)
    in_specs = ([pl.BlockSpec((TQ, 256), q_map)] + kv_specs(128) + kv_specs(256)
                + [pl.BlockSpec((TQ, 256), q_map)] + kv_specs(256) + kv_specs(512)
                + [pl.BlockSpec((4, DIFF_QK_DIM), const), pl.BlockSpec((1, 256), const)])
    args = [qa] + [ka] * n_parts + [va] * n_parts + [qb] + [kb] * n_parts + [vb] * n_parts
    return pl.pallas_call(
        functools.partial(_attn_kernel, n_parts=n_parts, lam_init=lam_init),
        grid=grid,
        in_specs=in_specs,
        out_specs=[pl.BlockSpec((TQ, 256), o_map), pl.BlockSpec((TQ, 256), o_map)],
        out_shape=[jax.ShapeDtypeStruct((rows, 256), BF16), jax.ShapeDtypeStruct((rows, 256), BF16)],
        compiler_params=_cparams(("parallel",) * len(grid)),
        name="attention_latent" if latent else "attention_context",
    )(*args, diff_lambda, diff_gain_lanes)


POOL_PAD = 16


def _pool(u_ref, n_seq, pw, ps, pa, pb, pc):
    pad = POOL_PAD
    n = n_seq + pad
    z = jnp.zeros((pad, 128), F32)
    first = lax.broadcasted_iota(jnp.int32, (1, 128), 1) < 64

    def shifted_sum(src, dst, k):
        dst[0:n, :] = src[0:n, :] + src[k:n + k, :]

    tiles = []
    for tile in range(2):
        pa[0:pad, :] = z
        pa[pad:n, :] = u_ref[0:n_seq, 128 * tile:128 * tile + 128]
        pa[n:n + pad, :] = z
        pb[n:n + pad, :] = z
        pc[n:n + pad, :] = z
        shifted_sum(pa, pb, 1)
        shifted_sum(pb, pc, 2)
        if tile == 0:
            w_lo = pb[pad - 1:pad - 1 + n_seq, :]
            w_hi = pc[pad - 2:pad - 2 + n_seq, :]
        else:
            shifted_sum(pc, pb, 4)
            w_lo = pb[pad - 4:pad - 4 + n_seq, :]
            shifted_sum(pb, pc, 8)
            w_hi = pc[pad - 8:pad - 8 + n_seq, :]
        tiles.append(jnp.where(first, w_lo, w_hi))
    wsum = jnp.concatenate(tiles, axis=1)
    grp = lax.broadcasted_iota(jnp.int32, (1, 256), 1) // 64
    back = jnp.where(grp == 0, 1, jnp.where(grp == 1, 2, jnp.where(grp == 2, 4, 8)))
    t = lax.broadcasted_iota(jnp.int32, (n_seq, 256), 0)
    count = jnp.minimum(t + (back - 1), n_seq - 1) - jnp.maximum(t - back, 0) + 1
    pooled = wsum / count.astype(F32) - u_ref[0:n_seq, 0:256]
    return _dot(pooled, pw) * ps


def _log_sigmoid(x):
    return jnp.minimum(x, 0.0) - jnp.log(1.0 + jnp.exp(-jnp.abs(x)))


def _mix_kernel(*refs, with_ctx):
    if with_ctx:
        (ul_ref, uc_ref, dl_ref, dh_ref, rg_ref, pw_ref, ps_ref,
         ycl_ref, ydl_ref, ycc_ref, ydc_ref, tst_ref, pa, pb, pc) = refs
    else:
        (ul_ref, uc_ref, dl_ref, dh_ref, rg_ref, pw_ref, ps_ref, ycl_ref, ydl_ref, tst_ref, pa, pb, pc) = refs
    c = RET_C
    n_chunks = SEQ // c
    pw = pw_ref[...]
    ps = ps_ref[...]

    ycl_ref[...] = _pool(ul_ref, SEQ, pw, ps, pa, pb, pc).astype(BF16)
    if with_ctx:
        ycc_ref[...] = _pool(uc_ref, CTX_LEN, pw, ps, pa, pb, pc).astype(BF16)

    lg = _log_sigmoid(dl_ref[...])
    lgh = _log_sigmoid(dh_ref[...])
    lgf, lgb = lg[0:1, :], lg[1:2, :]
    pos = lax.broadcasted_iota(jnp.int32, (c, 1), 0).astype(F32)
    wkf = jnp.exp((c - 1.0 - pos) * lgf)
    wkb = jnp.exp(pos * lgb)
    wqf = jnp.exp((pos + 1.0) * lgf)
    wqb = jnp.exp((c - pos) * lgb)
    dcf = jnp.exp(c * lgf)
    dcb = jnp.exp(c * lgb)
    same_head = _block_ones(256, HEAD_DIM)
    ones = same_head.astype(BF16)
    lane = lax.broadcasted_iota(jnp.int32, (1, 256), 1) // HEAD_DIM
    diff = (lax.broadcasted_iota(jnp.int32, (c, c), 0) - lax.broadcasted_iota(jnp.int32, (c, c), 1)).astype(F32)
    decay = []
    for h in range(4):
        df = jnp.exp(jnp.maximum(diff, 0.0) * lgh[h:h + 1, 0:1])
        db = jnp.exp(jnp.maximum(-diff, 0.0) * lgh[4 + h:5 + h, 0:1])
        decay.append(jnp.where(diff > 0, df, jnp.where(diff < 0, db, 2.0)))
    rg = rg_ref[...]
    k_scale = HEAD_DIM ** -0.5

    def load(ref, r0):
        rows = pl.ds(r0, c)
        return ref[rows, 256:512], ref[rows, 512:768] * k_scale, ref[rows, 768:1024], ref[rows, 1024:1280]

    def increment(k, v, wk):
        return jnp.where(same_head, _dot_tn(k * wk, v), 0.0)

    def chunk_out(q, k, v, gate, s_fwd, s_bwd):
        o = _dot(q * wqf, s_fwd) + _dot(q * wqb, s_bwd)
        for h in range(4):
            sel = lane == h
            s = _dot_nt(jnp.where(sel, q, 0.0), k)
            o = o + _dot(s * decay[h], jnp.where(sel, v, 0.0))
        on = o * lax.rsqrt(_group_sums(o * o, ones) * (1.0 / HEAD_DIM) + EPS) * rg
        return (on * _silu(gate)).astype(BF16)

    _, kc, vc, _ = load(uc_ref, 0)
    s_fwd = increment(kc, vc, wkf)
    s_bwd = increment(kc, vc, wkb)

    for j in range(n_chunks - 1, -1, -1):
        tst_ref[j] = s_bwd
        if j > 0:
            _, k, v, _ = load(ul_ref, j * c)
            s_bwd = s_bwd * dcb + increment(k, v, wkb)

    for j in range(n_chunks):
        q, k, v, gate = load(ul_ref, j * c)
        ydl_ref[pl.ds(j * c, c), :] = chunk_out(q, k, v, gate, s_fwd, tst_ref[j])
        if j < n_chunks - 1:
            s_fwd = s_fwd * dcf + increment(k, v, wkf)

    if with_ctx:
        qc, kc, vc, gc = load(uc_ref, 0)
        zeros = jnp.zeros((256, 256), F32)
        ydc_ref[...] = chunk_out(qc, kc, vc, gc, zeros, zeros)


def _mixer(ucd, decay_lanes, decay_heads, ret_gain_lanes, pool_w_bd, pool_scale, *, with_ctx):
    ctx_blk0 = T_LAT // CTX_LEN
    const = lambda b: (0, 0)
    out_specs = [pl.BlockSpec((SEQ, 256), lambda b: (b, 0))] * 2
    out_shape = [jax.ShapeDtypeStruct((T_LAT, 256), BF16)] * 2
    if with_ctx:
        out_specs = out_specs + [pl.BlockSpec((CTX_LEN, 256), lambda b: (b, 0))] * 2
        out_shape = out_shape + [jax.ShapeDtypeStruct((T_CTX, 256), BF16)] * 2
    return pl.pallas_call(
        functools.partial(_mix_kernel, with_ctx=with_ctx),
        grid=(BATCH,),
        in_specs=[
            pl.BlockSpec((SEQ, 1280), lambda b: (b, 0)),
            pl.BlockSpec((CTX_LEN, 1280), lambda b: (ctx_blk0 + b, 0)),
            pl.BlockSpec((2, 256), const),
            pl.BlockSpec((8, 128), const),
            pl.BlockSpec((1, 256), const),
            pl.BlockSpec((256, 256), const),
            pl.BlockSpec((1, 256), const),
        ],
        out_specs=out_specs,
        out_shape=out_shape,
        scratch_shapes=[pltpu.VMEM((SEQ // RET_C, 256, 256), F32)]
        + [pltpu.VMEM((SEQ + 2 * POOL_PAD, 128), F32)] * 3,
        compiler_params=_cparams(("parallel",)),
        name="pool_retention_mixer",
    )(ucd, ucd, decay_lanes, decay_heads, ret_gain_lanes, pool_w_bd, pool_scale)


def _rope_tables(dim, tm):
    q = dim // 4
    n_rows = SEQ // GRID_W
    rows = jnp.repeat(jnp.arange(n_rows), GRID_W).astype(F32)
    cols = jnp.tile(jnp.arange(GRID_W), n_rows).astype(F32)
    inv = ROPE_BASE ** (-jnp.arange(q, dtype=F32) / q)
    ar, ac = rows[:, None] * inv, cols[:, None] * inv
    cos = jnp.concatenate([jnp.cos(ar)] * 2 + [jnp.cos(ac)] * 2, axis=-1)
    sin = jnp.concatenate([jnp.sin(ar)] * 2 + [jnp.sin(ac)] * 2, axis=-1)
    first_half = jnp.tile(jnp.repeat(jnp.array([True, False]), q), 2)
    tabs = jnp.stack([cos, jnp.where(first_half, -sin, 0.0), jnp.where(first_half, 0.0, sin)])
    tabs = jnp.tile(tabs, (1, 1, 128 // dim))
    ident = jnp.stack([jnp.ones((tm, 128), F32), jnp.zeros((tm, 128), F32), jnp.zeros((tm, 128), F32)])
    return jnp.concatenate([tabs, ident], axis=1)


def _block_diag(blocks):
    n, d, _ = blocks.shape
    eye = jnp.eye(n, dtype=blocks.dtype)
    return jnp.einsum('gcd,gh->gchd', blocks, eye).reshape(n * d, n * d)


def kernel(x, c, ctx, c_ctx, w_mod, b_mod, norm_gain, ffn_w_gate, ffn_w_up, ffn_w_down, w_in, w_out,
           attn_qk_gain, diff_lambda, diff_out_gain, pool_w, pool_scale, ret_decay_logit, ret_out_gain):
    xs = (x.reshape(T_LAT, D_MODEL), ctx.reshape(T_CTX, D_MODEL))
    cc = jnp.concatenate([c, c_ctx[None, :], jnp.zeros((MOD_ROWS - BATCH - 1, D_MODEL), F32)], axis=0)
    mod_all = _modulation(cc, w_mod, b_mod).reshape(DEPTH, MOD_ROWS, 9, D_MODEL)
    tab_a = _rope_tables(HEAD_DIM, TM_PROJ)
    tab_b = _rope_tables(DIFF_QK_DIM, TM_PROJ)

    for i in range(DEPTH):
        last = i == DEPTH - 1
        mod = mod_all[i]
        gains = norm_gain[i]
        qk_gain_lanes = jnp.tile(attn_qk_gain[i], (1, 256 // HEAD_DIM))
        diff_gain_lanes = jnp.tile(diff_out_gain[i], 256 // HEAD_DIM)[None, :]
        ret_gain_lanes = jnp.tile(ret_out_gain[i], 256 // HEAD_DIM)[None, :]
        decay_lanes = jnp.repeat(ret_decay_logit[i], HEAD_DIM, axis=1)
        decay_heads = jnp.broadcast_to(ret_decay_logit[i].reshape(8, 1), (8, 128))
        pool_w_bd = _block_diag(pool_w[i])
        lam_init = 0.8 - 0.6 * math.exp(-0.3 * i)

        def ffn(hs, sub, rows, **mix):
            return _ffn_sublayer(hs, mod, gains, ffn_w_gate, ffn_w_up, ffn_w_down, layer=i, sub=sub, rows=rows,
                                 **mix)

        h = ffn(xs, 0, T_ALL)
        qa, ka, va, qb, kb, vb, ucd = _in_projection(h, mod, gains, w_in, tab_a, tab_b, qk_gain_lanes, layer=i)
        ya, yb = _attention(qa, ka, va, qb, kb, vb, diff_lambda[i], diff_gain_lanes, lam_init=lam_init, latent=True)
        mixed = _mixer(ucd, decay_lanes, decay_heads, ret_gain_lanes, pool_w_bd, pool_scale[i][None, :],
                       with_ctx=not last)
        if last:
            ys, rows = [(ya,), (yb,), (mixed[0],), (mixed[1],)], T_LAT
        else:
            ya_c, yb_c = _attention(qa, ka, va, qb, kb, vb, diff_lambda[i], diff_gain_lanes, lam_init=lam_init,
                                    latent=False)
            ys, rows = [(ya, ya_c), (yb, yb_c), (mixed[0], mixed[2]), (mixed[1], mixed[3])], T_ALL
        h = ffn((h,), 2, rows, ys=ys, w_out=w_out)
        xs = (h,)
    return h.reshape(BATCH, SEQ, D_MODEL)
```

```python
import functools
import math

import jax
import jax.numpy as jnp
from jax import lax
from jax.experimental import pallas as pl
from jax.experimental.pallas import tpu as pltpu

F32 = jnp.float32
BF16 = jnp.bfloat16

D_MODEL = 1024
BATCH = 8
SEQ = 2048
DEPTH = 2
CTX_LEN = 256
GRID_W = 64
HEAD_DIM = 64
DIFF_QK_DIM = 32
GROUP_WIDTH = 256
D_FF = 2816
D_IN = 2560
FFN_RESIDUAL = 0.5
ROPE_BASE = 10000.0
EPS = 1e-6

T_LAT = BATCH * SEQ
T_CTX = BATCH * CTX_LEN
T_ALL = T_LAT + T_CTX
MOD_ROWS = 16

TM_FFN = 512
FFN_SPLIT = 2
TM_PROJ = 1024
PROJ_SPLIT = 4
LOG2E = 1.4426950408889634
TQ = 256
RET_C = 256
VMEM_LIMIT = 56 * 1024 * 1024


def _cparams(sem):
    return pltpu.CompilerParams(dimension_semantics=sem, vmem_limit_bytes=VMEM_LIMIT)


def _silu(x):
    return x * jax.nn.sigmoid(x)


_MXU = dict(preferred_element_type=F32, precision=lax.Precision.DEFAULT)


def _dot(a, b):
    return jnp.dot(a, b, **_MXU)


def _dot_nt(a, b):
    return lax.dot_general(a, b, (((1,), (1,)), ((), ())), **_MXU)


def _dot_tn(a, b):
    return lax.dot_general(a, b, (((0,), (0,)), ((), ())), **_MXU)


def _block_ones(n, blk):
    r = lax.broadcasted_iota(jnp.int32, (n, n), 0) // blk
    c = lax.broadcasted_iota(jnp.int32, (n, n), 1) // blk
    return r == c


def _group_sums(sq, ones):
    return _dot(sq, ones)


def _rms_rows(x):
    return lax.rsqrt(jnp.mean(x * x, axis=-1, keepdims=True) + EPS)


def _mod_row_index(tile, tm):
    return jnp.minimum((tile * tm) // SEQ, BATCH)


def _mod_kernel(cc_ref, w_ref, b_ref, o_ref):
    o_ref[...] = _dot(_silu(cc_ref[...]), w_ref[...]) + b_ref[...]


def _modulation(cc, w_mod, b_mod):
    tn = 1024
    n_out = 9 * D_MODEL
    return pl.pallas_call(
        _mod_kernel,
        grid=(DEPTH, n_out // tn),
        in_specs=[
            pl.BlockSpec((MOD_ROWS, D_MODEL), lambda l, j: (0, 0)),
            pl.BlockSpec((None, D_MODEL, tn), lambda l, j: (l, 0, j)),
            pl.BlockSpec((None, 1, tn), lambda l, j: (l, 0, j)),
        ],
        out_specs=pl.BlockSpec((None, MOD_ROWS, tn), lambda l, j: (l, 0, j)),
        out_shape=jax.ShapeDtypeStruct((DEPTH, MOD_ROWS, n_out), F32),
        compiler_params=_cparams(("parallel", "parallel")),
        name="adaln_modulation",
    )(cc, w_mod, b_mod.reshape(DEPTH, 1, n_out))


def _token_tile(refs, rows, lat_tiles):
    if len(refs) == 1:
        return refs[0][rows, :]
    return jnp.where(pl.program_id(0) < lat_tiles, refs[0][rows, :], refs[1][rows, :])


def _ffn_kernel(*refs, n_x, n_y, sub, g_pre, g_post, lat_tiles):
    x_refs = refs[:n_x]
    m_ref, g_ref = refs[n_x:n_x + 2]
    y_refs = refs[n_x + 2:n_x + 2 + 4 * n_y]
    wo_ref = refs[n_x + 2 + 4 * n_y] if n_y else None
    wg_ref, wu_ref, wd_ref, o_ref = refs[-4:]
    pre = g_ref[g_pre:g_pre + 1, :] * (1.0 + m_ref[3 * sub + 1:3 * sub + 2, :])
    shift = m_ref[3 * sub:3 * sub + 1, :]
    post = (FFN_RESIDUAL * m_ref[3 * sub + 2:3 * sub + 3, :]) * g_ref[g_post:g_post + 1, :]
    mix_post = m_ref[5:6, :] * g_ref[3:4, :]
    rs = o_ref.shape[0] // FFN_SPLIT
    for s in range(FFN_SPLIT):
        rows = pl.ds(s * rs, rs)
        x = _token_tile(x_refs, rows, lat_tiles)
        if n_y:
            mixed = jnp.concatenate([_token_tile(y_refs[n_y * j:n_y * (j + 1)], rows, lat_tiles) for j in range(4)],
                                    axis=1)
            y = _dot(mixed.astype(F32), wo_ref[...])
            x = x + (y * _rms_rows(y)) * mix_post
        hm = x * _rms_rows(x) * pre + shift
        gate = _dot(hm, wg_ref[...])
        up = _dot(hm, wu_ref[...])
        y = _dot(_silu(gate) * up, wd_ref[...])
        o_ref[rows, :] = x + (y * _rms_rows(y)) * post


def _stream_specs(xs, tm, width):
    if len(xs) == 1:
        return [pl.BlockSpec((tm, width), lambda i: (i, 0))]
    lat_tiles = xs[0].shape[0] // tm
    return [pl.BlockSpec((tm, width), lambda i: (jnp.minimum(i, lat_tiles - 1), 0)),
            pl.BlockSpec((tm, width), lambda i: (jnp.maximum(i - lat_tiles, 0), 0))]


def _ffn_sublayer(xs, mod, gains, wg, wu, wd, *, layer, sub, rows, ys=(), w_out=None):
    tm = TM_FFN
    n_y = len(ys[0]) if ys else 0
    kern = functools.partial(_ffn_kernel, n_x=len(xs), n_y=n_y, sub=sub, g_pre=2 * sub, g_post=2 * sub + 1,
                             lat_tiles=T_LAT // tm)
    k = sub // 2
    resident = dict(pipeline_mode=pl.Buffered(1))
    y_specs, y_args = [], []
    for parts in ys:
        y_specs += _stream_specs(parts, tm, 256)
        y_args += list(parts)
    if ys:
        y_specs.append(pl.BlockSpec((None, D_MODEL, D_MODEL), lambda i: (layer, 0, 0), **resident))
        y_args.append(w_out)
    return pl.pallas_call(
        kern,
        grid=(rows // tm,),
        in_specs=_stream_specs(xs, tm, D_MODEL) + [
            pl.BlockSpec((None, 9, D_MODEL), lambda i: (_mod_row_index(i, tm), 0, 0)),
            pl.BlockSpec((6, D_MODEL), lambda i: (0, 0)),
        ] + y_specs + [
            pl.BlockSpec((None, None, D_MODEL, D_FF), lambda i: (layer, k, 0, 0), **resident),
            pl.BlockSpec((None, None, D_MODEL, D_FF), lambda i: (layer, k, 0, 0), **resident),
            pl.BlockSpec((None, None, D_FF, D_MODEL), lambda i: (layer, k, 0, 0), **resident),
        ],
        out_specs=pl.BlockSpec((tm, D_MODEL), lambda i: (i, 0)),
        out_shape=jax.ShapeDtypeStruct((rows, D_MODEL), F32),
        compiler_params=_cparams(("parallel",)),
        name=f"ffn_sublayer_{sub}",
    )(*xs, mod, gains, *y_args, wg, wu, wd)


def _rope(x, tab_ref, rows, quarter):
    w = x.shape[1]
    reps = w // tab_ref.shape[2]

    def tab(k):
        t = tab_ref[k, rows, :]
        return t if reps == 1 else jnp.concatenate([t] * reps, axis=1)

    return x * tab(0) + pltpu.roll(x, w - quarter, 1) * tab(1) + pltpu.roll(x, quarter, 1) * tab(2)


def _swap_middle_heads(x):
    first = lax.broadcasted_iota(jnp.int32, (1, 128), 1) < HEAD_DIM
    lo, hi = x[:, 0:128], x[:, 128:256]
    return jnp.concatenate([jnp.where(first, lo, pltpu.roll(hi, HEAD_DIM, 1)),
                            jnp.where(first, pltpu.roll(lo, HEAD_DIM, 1), hi)], axis=1)


def _inproj_kernel(x_ref, m_ref, g_ref, w_ref, ta_ref, tb_ref, qkg_ref,
                   qa_ref, ka_ref, va_ref, qb_ref, kb_ref, vb_ref, ucd_ref):
    pre = g_ref[2:3, :] * (1.0 + m_ref[4:5, :])
    shift = m_ref[3:4, :]
    ones = _block_ones(256, HEAD_DIM).astype(F32)
    inv_d = 1.0 / HEAD_DIM
    lane128 = lax.broadcasted_iota(jnp.int32, (1, 128), 1)
    rs = x_ref.shape[0] // PROJ_SPLIT
    for s in range(PROJ_SPLIT):
        rows = pl.ds(s * rs, rs)
        x = x_ref[rows, :]
        hm = x * _rms_rows(x) * pre + shift

        def proj(lo, hi):
            return _dot(hm, w_ref[:, lo:hi])

        q = proj(0, 256)
        q = q * lax.rsqrt(_group_sums(q * q, ones) * inv_d + EPS) * qkg_ref[0:1, :]
        q = _rope(q, ta_ref, rows, HEAD_DIM // 4) * (HEAD_DIM ** -0.5 * LOG2E)
        qa_ref[rows, :] = _swap_middle_heads(q).astype(BF16)
        k = proj(256, 384)
        k = k * lax.rsqrt(_group_sums(k * k, ones[:128, :128]) * inv_d + EPS) * qkg_ref[1:2, :128]
        ka_ref[rows, :] = _rope(k, ta_ref, rows, HEAD_DIM // 4).astype(BF16)
        v = proj(384, 512)
        for g in range(2):
            va_ref[rows, 128 * g:128 * g + 128] = jnp.where(lane128 == HEAD_DIM * (1 - g), 1.0, v).astype(BF16)

        qb_ref[rows, :] = (_rope(proj(512, 768), tb_ref, rows, DIFF_QK_DIM // 4)
                           * (DIFF_QK_DIM ** -0.5 * LOG2E)).astype(BF16)
        kb_ref[rows, :] = _rope(proj(768, 1024), tb_ref, rows, DIFF_QK_DIM // 4).astype(BF16)
        v = proj(1024, 1280)
        for hd in range(4):
            own = (lane128 // HEAD_DIM) == hd % 2
            ones_col = jnp.where(lane128 == HEAD_DIM * (1 - hd % 2), 1.0, 0.0)
            pair = v[:, 128 * (hd // 2):128 * (hd // 2) + 128]
            vb_ref[rows, 128 * hd:128 * hd + 128] = jnp.where(own, pair, ones_col).astype(BF16)

        ucd_ref[rows, :] = proj(1280, 2560)


def _in_projection(h, mod, gains, w_in, tab_a, tab_b, qk_gain_lanes, *, layer):
    tm = TM_PROJ
    lat_tiles = T_LAT // tm
    tiles_per_seq = SEQ // tm

    def tab_idx(i):
        return jnp.where(i < lat_tiles, i % tiles_per_seq, tiles_per_seq)

    def row_spec(width):
        return pl.BlockSpec((tm, width), lambda i: (i, 0))

    def out(width, dtype):
        return jax.ShapeDtypeStruct((T_ALL, width), dtype)

    return pl.pallas_call(
        _inproj_kernel,
        grid=(T_ALL // tm,),
        in_specs=[
            row_spec(D_MODEL),
            pl.BlockSpec((None, 9, D_MODEL), lambda i: (_mod_row_index(i, tm), 0, 0)),
            pl.BlockSpec((6, D_MODEL), lambda i: (0, 0)),
            pl.BlockSpec((None, D_MODEL, D_IN), lambda i: (layer, 0, 0), pipeline_mode=pl.Buffered(1)),
            pl.BlockSpec((3, tm, 128), lambda i: (0, tab_idx(i), 0)),
            pl.BlockSpec((3, tm, 128), lambda i: (0, tab_idx(i), 0)),
            pl.BlockSpec((2, 256), lambda i: (0, 0)),
        ],
        out_specs=[row_spec(256), row_spec(128), row_spec(256), row_spec(256), row_spec(256), row_spec(512),
                   row_spec(1280)],
        out_shape=[out(256, BF16), out(128, BF16), out(256, BF16), out(256, BF16), out(256, BF16), out(512, BF16),
                   out(1280, F32)],
        compiler_params=_cparams(("parallel",)),
        name="in_projection",
    )(h, mod, gains, w_in, tab_a, tab_b, qk_gain_lanes)


def _softmax_pv(qs, k_refs, v_refs, lanes, ones_lane):
    return _prob_values(_probabilities(qs, k_refs), v_refs, lanes, ones_lane)


def _probabilities(qs, k_refs):
    ss = [_dot_nt(qs, k[...]) for k in k_refs]
    m = ss[0].max(axis=-1, keepdims=True)
    for s in ss[1:]:
        m = jnp.maximum(m, s.max(axis=-1, keepdims=True))
    return [jnp.exp2(s - m).astype(BF16) for s in ss]


def _prob_values(ps, v_refs, lanes, ones_lane):
    half = ps[0].shape[0] // 2
    outs = []
    for r0 in (0, half):
        o = None
        for p, v in zip(ps, v_refs):
            pv = _dot(p[r0:r0 + half], v[:, lanes])
            o = pv if o is None else o + pv
        outs.append(o / o[:, ones_lane:ones_lane + 1])
    return outs


def _attn_kernel(*refs, n_parts, lam_init):
    qa_ref = refs[0]
    ka = refs[1:1 + n_parts]
    va = refs[1 + n_parts:1 + 2 * n_parts]
    qb_ref = refs[1 + 2 * n_parts]
    kb = refs[2 + 2 * n_parts:2 + 3 * n_parts]
    vb = refs[2 + 3 * n_parts:2 + 4 * n_parts]
    lam_ref, dg_ref, ya_ref, yb_ref = refs[2 + 4 * n_parts:]

    lane128 = lax.broadcasted_iota(jnp.int32, (1, 128), 1)
    first = lane128 < HEAD_DIM
    qa = qa_ref[...]
    zero = jnp.zeros((), BF16)
    outs = []
    for g in range(2):
        sel = (lane128 // HEAD_DIM) == g
        qs = jnp.concatenate([jnp.where(sel, qa[:, 0:128], zero), jnp.where(sel, qa[:, 128:256], zero)], axis=0)
        outs.append(_softmax_pv(qs, ka, va, slice(128 * g, 128 * g + 128), HEAD_DIM * (1 - g)))
    y0 = jnp.where(first, outs[0][0], outs[1][0])
    y1 = jnp.where(first, outs[0][1], outs[1][1])
    ya_ref[...] = _swap_middle_heads(jnp.concatenate([y0, y1], axis=1)).astype(BF16)

    lp = lam_ref[...]
    lam = (jnp.exp(jnp.sum(lp[0:1] * lp[1:2], axis=-1, keepdims=True))
           - jnp.exp(jnp.sum(lp[2:3] * lp[3:4], axis=-1, keepdims=True)) + lam_init)
    half = lax.broadcasted_iota(jnp.int32, (1, 256), 1) // DIFF_QK_DIM
    qb = qb_ref[...]
    d = []
    for h in range(4):
        qs = jnp.concatenate([jnp.where(half == 2 * h, qb, zero), jnp.where(half == 2 * h + 1, qb, zero)], axis=0)
        o1, o2 = _softmax_pv(qs, kb, vb, slice(128 * h, 128 * h + 128), HEAD_DIM * (1 - h % 2))
        d.append(o1 - lam * o2)
    y = jnp.concatenate([jnp.where(first, d[0], d[1]), jnp.where(first, d[2], d[3])], axis=1)
    ones = _block_ones(256, HEAD_DIM).astype(F32)
    yn = y * lax.rsqrt(_group_sums(y * y, ones) * (1.0 / HEAD_DIM) + EPS) * dg_ref[...]
    yb_ref[...] = (yn * (1.0 - lam_init)).astype(BF16)


def _attention(qa, ka, va, qb, kb, vb, diff_lambda, diff_gain_lanes, *, lam_init, latent):
    ctx_blk0 = T_LAT // CTX_LEN
    if latent:
        grid = (BATCH, SEQ // TQ)
        q_map = lambda b, i: (b * (SEQ // TQ) + i, 0)
        kv_specs = lambda w: [pl.BlockSpec((CTX_LEN, w), lambda b, i: (ctx_blk0 + b, 0)),
                              pl.BlockSpec((SEQ, w), lambda b, i: (b, 0))]
        rows = T_LAT
        n_parts = 2
        const = lambda b, i: (0, 0)
        o_map = q_map
    else:
        grid = (BATCH,)
        q_map = lambda b: (ctx_blk0 + b, 0)
        kv_specs = lambda w: [pl.BlockSpec((CTX_LEN, w), lambda b: (ctx_blk0 + b, 0))]
        rows = T_CTX
        n_parts = 1
        const = lambda b: (0, 0)
        o_map = lambda b: (b, 0)
    in_specs = ([pl.BlockSpec((TQ, 256), q_map)] + kv_specs(128) + kv_specs(256)
                + [pl.BlockSpec((TQ, 256), q_map)] + kv_specs(256) + kv_specs(512)
                + [pl.BlockSpec((4, DIFF_QK_DIM), const), pl.BlockSpec((1, 256), const)])
    args = [qa] + [ka] * n_parts + [va] * n_parts + [qb] + [kb] * n_parts + [vb] * n_parts
    return pl.pallas_call(
        functools.partial(_attn_kernel, n_parts=n_parts, lam_init=lam_init),
        grid=grid,
        in_specs=in_specs,
        out_specs=[pl.BlockSpec((TQ, 256), o_map), pl.BlockSpec((TQ, 256), o_map)],
        out_shape=[jax.ShapeDtypeStruct((rows, 256), BF16), jax.ShapeDtypeStruct((rows, 256), BF16)],
        compiler_params=_cparams(("parallel",) * len(grid)),
        name="attention_latent" if latent else "attention_context",
    )(*args, diff_lambda, diff_gain_lanes)


POOL_PAD = 16


def _pool(u_ref, n_seq, pw, ps, pa, pb, pc):
    pad = POOL_PAD
    n = n_seq + pad
    z = jnp.zeros((pad, 128), F32)
    first = lax.broadcasted_iota(jnp.int32, (1, 128), 1) < 64

    def shifted_sum(src, dst, k):
        dst[0:n, :] = src[0:n, :] + src[k:n + k, :]

    tiles = []
    for tile in range(2):
        pa[0:pad, :] = z
        pa[pad:n, :] = u_ref[0:n_seq, 128 * tile:128 * tile + 128]
        pa[n:n + pad, :] = z
        pb[n:n + pad, :] = z
        pc[n:n + pad, :] = z
        shifted_sum(pa, pb, 1)
        shifted_sum(pb, pc, 2)
        if tile == 0:
            w_lo = pb[pad - 1:pad - 1 + n_seq, :]
            w_hi = pc[pad - 2:pad - 2 + n_seq, :]
        else:
            shifted_sum(pc, pb, 4)
            w_lo = pb[pad - 4:pad - 4 + n_seq, :]
            shifted_sum(pb, pc, 8)
            w_hi = pc[pad - 8:pad - 8 + n_seq, :]
        tiles.append(jnp.where(first, w_lo, w_hi))
    wsum = jnp.concatenate(tiles, axis=1)
    grp = lax.broadcasted_iota(jnp.int32, (1, 256), 1) // 64
    back = jnp.where(grp == 0, 1, jnp.where(grp == 1, 2, jnp.where(grp == 2, 4, 8)))
    t = lax.broadcasted_iota(jnp.int32, (n_seq, 256), 0)
    count = jnp.minimum(t + (back - 1), n_seq - 1) - jnp.maximum(t - back, 0) + 1
    pooled = wsum / count.astype(F32) - u_ref[0:n_seq, 0:256]
    return _dot(pooled, pw) * ps


def _log_sigmoid(x):
    return jnp.minimum(x, 0.0) - jnp.log(1.0 + jnp.exp(-jnp.abs(x)))


def _mix_kernel(*refs, with_ctx):
    if with_ctx:
        (ul_ref, uc_ref, dl_ref, dh_ref, rg_ref, pw_ref, ps_ref,
         ycl_ref, ydl_ref, ycc_ref, ydc_ref, tst_ref, pa, pb, pc) = refs
    else:
        (ul_ref, uc_ref, dl_ref, dh_ref, rg_ref, pw_ref, ps_ref, ycl_ref, ydl_ref, tst_ref, pa, pb, pc) = refs
    c = RET_C
    n_chunks = SEQ // c
    pw = pw_ref[...]
    ps = ps_ref[...]

    ycl_ref[...] = _pool(ul_ref, SEQ, pw, ps, pa, pb, pc).astype(BF16)
    if with_ctx:
        ycc_ref[...] = _pool(uc_ref, CTX_LEN, pw, ps, pa, pb, pc).astype(BF16)

    lg = _log_sigmoid(dl_ref[...])
    lgh = _log_sigmoid(dh_ref[...])
    lgf, lgb = lg[0:1, :], lg[1:2, :]
    pos = lax.broadcasted_iota(jnp.int32, (c, 1), 0).astype(F32)
    wkf = jnp.exp((c - 1.0 - pos) * lgf)
    wkb = jnp.exp(pos * lgb)
    wqf = jnp.exp((pos + 1.0) * lgf)
    wqb = jnp.exp((c - pos) * lgb)
    dcf = jnp.exp(c * lgf)
    dcb = jnp.exp(c * lgb)
    same_head = _block_ones(256, HEAD_DIM)
    ones = same_head.astype(F32)
    lane = lax.broadcasted_iota(jnp.int32, (1, 256), 1) // HEAD_DIM
    diff = (lax.broadcasted_iota(jnp.int32, (c, c), 0) - lax.broadcasted_iota(jnp.int32, (c, c), 1)).astype(F32)
    decay = []
    for h in range(4):
        df = jnp.exp(jnp.maximum(diff, 0.0) * lgh[h:h + 1, 0:1])
        db = jnp.exp(jnp.maximum(-diff, 0.0) * lgh[4 + h:5 + h, 0:1])
        decay.append(jnp.where(diff > 0, df, jnp.where(diff < 0, db, 2.0)))
    rg = rg_ref[...]
    k_scale = HEAD_DIM ** -0.5

    def load(ref, r0):
        rows = pl.ds(r0, c)
        return ref[rows, 256:512], ref[rows, 512:768] * k_scale, ref[rows, 768:1024], ref[rows, 1024:1280]

    def increment(k, v, wk):
        return jnp.where(same_head, _dot_tn(k * wk, v), 0.0)

    def chunk_out(q, k, v, gate, s_fwd, s_bwd):
        o = _dot(q * wqf, s_fwd) + _dot(q * wqb, s_bwd)
        for h in range(4):
            sel = lane == h
            s = _dot_nt(jnp.where(sel, q, 0.0), k)
            o = o + _dot(s * decay[h], jnp.where(sel, v, 0.0))
        on = o * lax.rsqrt(_group_sums(o * o, ones) * (1.0 / HEAD_DIM) + EPS) * rg
        return (on * _silu(gate)).astype(BF16)

    _, kc, vc, _ = load(uc_ref, 0)
    s_fwd = increment(kc, vc, wkf)
    s_bwd = increment(kc, vc, wkb)

    for j in range(n_chunks - 1, -1, -1):
        tst_ref[j] = s_bwd
        if j > 0:
            _, k, v, _ = load(ul_ref, j * c)
            s_bwd = s_bwd * dcb + increment(k, v, wkb)

    for j in range(n_chunks):
        q, k, v, gate = load(ul_ref, j * c)
        ydl_ref[pl.ds(j * c, c), :] = chunk_out(q, k, v, gate, s_fwd, tst_ref[j])
        if j < n_chunks - 1:
            s_fwd = s_fwd * dcf + increment(k, v, wkf)

    if with_ctx:
        qc, kc, vc, gc = load(uc_ref, 0)
        zeros = jnp.zeros((256, 256), F32)
        ydc_ref[...] = chunk_out(qc, kc, vc, gc, zeros, zeros)


def _mixer(ucd, decay_lanes, decay_heads, ret_gain_lanes, pool_w_bd, pool_scale, *, with_ctx):
    ctx_blk0 = T_LAT // CTX_LEN
    const = lambda b: (0, 0)
    out_specs = [pl.BlockSpec((SEQ, 256), lambda b: (b, 0))] * 2
    out_shape = [jax.ShapeDtypeStruct((T_LAT, 256), BF16)] * 2
    if with_ctx:
        out_specs = out_specs + [pl.BlockSpec((CTX_LEN, 256), lambda b: (b, 0))] * 2
        out_shape = out_shape + [jax.ShapeDtypeStruct((T_CTX, 256), BF16)] * 2
    return pl.pallas_call(
        functools.partial(_mix_kernel, with_ctx=with_ctx),
        grid=(BATCH,),
        in_specs=[
            pl.BlockSpec((SEQ, 1280), lambda b: (b, 0)),
            pl.BlockSpec((CTX_LEN, 1280), lambda b: (ctx_blk0 + b, 0)),
            pl.BlockSpec((2, 256), const),
            pl.BlockSpec((8, 128), const),
            pl.BlockSpec((1, 256), const),
            pl.BlockSpec((256, 256), const),
            pl.BlockSpec((1, 256), const),
        ],
        out_specs=out_specs,
        out_shape=out_shape,
        scratch_shapes=[pltpu.VMEM((SEQ // RET_C, 256, 256), F32)]
        + [pltpu.VMEM((SEQ + 2 * POOL_PAD, 128), F32)] * 3,
        compiler_params=_cparams(("parallel",)),
        name="pool_retention_mixer",
    )(ucd, ucd, decay_lanes, decay_heads, ret_gain_lanes, pool_w_bd, pool_scale)


def _rope_tables(dim, tm):
    q = dim // 4
    n_rows = SEQ // GRID_W
    rows = jnp.repeat(jnp.arange(n_rows), GRID_W).astype(F32)
    cols = jnp.tile(jnp.arange(GRID_W), n_rows).astype(F32)
    inv = ROPE_BASE ** (-jnp.arange(q, dtype=F32) / q)
    ar, ac = rows[:, None] * inv, cols[:, None] * inv
    cos = jnp.concatenate([jnp.cos(ar)] * 2 + [jnp.cos(ac)] * 2, axis=-1)
    sin = jnp.concatenate([jnp.sin(ar)] * 2 + [jnp.sin(ac)] * 2, axis=-1)
    first_half = jnp.tile(jnp.repeat(jnp.array([True, False]), q), 2)
    tabs = jnp.stack([cos, jnp.where(first_half, -sin, 0.0), jnp.where(first_half, 0.0, sin)])
    tabs = jnp.tile(tabs, (1, 1, 128 // dim))
    ident = jnp.stack([jnp.ones((tm, 128), F32), jnp.zeros((tm, 128), F32), jnp.zeros((tm, 128), F32)])
    return jnp.concatenate([tabs, ident], axis=1)


def _block_diag(blocks):
    n, d, _ = blocks.shape
    eye = jnp.eye(n, dtype=blocks.dtype)
    return jnp.einsum('gcd,gh->gchd', blocks, eye).reshape(n * d, n * d)


def kernel(x, c, ctx, c_ctx, w_mod, b_mod, norm_gain, ffn_w_gate, ffn_w_up, ffn_w_down, w_in, w_out,
           attn_qk_gain, diff_lambda, diff_out_gain, pool_w, pool_scale, ret_decay_logit, ret_out_gain):
    xs = (x.reshape(T_LAT, D_MODEL), ctx.reshape(T_CTX, D_MODEL))
    cc = jnp.concatenate([c, c_ctx[None, :], jnp.zeros((MOD_ROWS - BATCH - 1, D_MODEL), F32)], axis=0)
    mod_all = _modulation(cc, w_mod, b_mod).reshape(DEPTH, MOD_ROWS, 9, D_MODEL)
    tab_a = _rope_tables(HEAD_DIM, TM_PROJ)
    tab_b = _rope_tables(DIFF_QK_DIM, TM_PROJ)

    for i in range(DEPTH):
        last = i == DEPTH - 1
        mod = mod_all[i]
        gains = norm_gain[i]
        qk_gain_lanes = jnp.tile(attn_qk_gain[i], (1, 256 // HEAD_DIM))
        diff_gain_lanes = jnp.tile(diff_out_gain[i], 256 // HEAD_DIM)[None, :]
        ret_gain_lanes = jnp.tile(ret_out_gain[i], 256 // HEAD_DIM)[None, :]
        decay_lanes = jnp.repeat(ret_decay_logit[i], HEAD_DIM, axis=1)
        decay_heads = jnp.broadcast_to(ret_decay_logit[i].reshape(8, 1), (8, 128))
        pool_w_bd = _block_diag(pool_w[i])
        lam_init = 0.8 - 0.6 * math.exp(-0.3 * i)

        def ffn(hs, sub, rows, **mix):
            return _ffn_sublayer(hs, mod, gains, ffn_w_gate, ffn_w_up, ffn_w_down, layer=i, sub=sub, rows=rows,
                                 **mix)

        h = ffn(xs, 0, T_ALL)
        qa, ka, va, qb, kb, vb, ucd = _in_projection(h, mod, gains, w_in, tab_a, tab_b, qk_gain_lanes, layer=i)
        ya, yb = _attention(qa, ka, va, qb, kb, vb, diff_lambda[i], diff_gain_lanes, lam_init=lam_init, latent=True)
        mixed = _mixer(ucd, decay_lanes, decay_heads, ret_gain_lanes, pool_w_bd, pool_scale[i][None, :],
                       with_ctx=not last)
        if last:
            ys, rows = [(ya,), (yb,), (mixed[0],), (mixed[1],)], T_LAT
        else:
            ya_c, yb_c = _attention(qa, ka, va, qb, kb, vb, diff_lambda[i], diff_gain_lanes, lam_init=lam_init,
                                    latent=False)
            ys, rows = [(ya, ya_c), (yb, yb_c), (mixed[0], mixed[2]), (mixed[1], mixed[3])], T_ALL
        h = ffn((h,), 2, rows, ys=ys, w_out=w_out)
        xs = (h,)
    return h.reshape(BATCH, SEQ, D_MODEL)
```

```python
import functools
import math

import jax
import jax.numpy as jnp
import numpy as np
from jax import lax
from jax.experimental import pallas as pl
from jax.experimental.pallas import tpu as pltpu

F32 = jnp.float32
BF16 = jnp.bfloat16

D_MODEL = 1024
BATCH = 8
SEQ = 2048
DEPTH = 2
CTX_LEN = 256
GRID_W = 64
HEAD_DIM = 64
DIFF_QK_DIM = 32
GROUP_WIDTH = 256
D_FF = 2816
D_IN = 2560
FFN_RESIDUAL = 0.5
ROPE_BASE = 10000.0
EPS = 1e-6

T_LAT = BATCH * SEQ
T_CTX = BATCH * CTX_LEN
T_ALL = T_LAT + T_CTX
MOD_ROWS = 16

TM_FFN = 512
FFN_SPLIT = 2
TM_PROJ = 1024
PROJ_SPLIT = 4
LOG2E = 1.4426950408889634
TQ = 256
RET_C = 256
VMEM_LIMIT = 56 * 1024 * 1024


def _cparams(sem):
    return pltpu.CompilerParams(dimension_semantics=sem, vmem_limit_bytes=VMEM_LIMIT)


def _silu(x):
    return x * jax.nn.sigmoid(x)


_MXU = dict(preferred_element_type=F32, precision=lax.Precision.DEFAULT)


def _dot(a, b):
    return jnp.dot(a, b, **_MXU)


def _dot_nt(a, b):
    return lax.dot_general(a, b, (((1,), (1,)), ((), ())), **_MXU)


def _dot_tn(a, b):
    return lax.dot_general(a, b, (((0,), (0,)), ((), ())), **_MXU)


def _block_ones(n, blk):
    r = lax.broadcasted_iota(jnp.int32, (n, n), 0) // blk
    c = lax.broadcasted_iota(jnp.int32, (n, n), 1) // blk
    return r == c


def _group_sums(sq, ones):
    return _dot(sq, ones)


def _rms_rows(x):
    return lax.rsqrt(jnp.mean(x * x, axis=-1, keepdims=True) + EPS)


def _mod_row_index(tile, tm):
    return jnp.minimum((tile * tm) // SEQ, BATCH)


def _mod_kernel(cc_ref, w_ref, b_ref, o_ref):
    o_ref[...] = _dot(_silu(cc_ref[...]), w_ref[...]) + b_ref[...]


def _modulation(cc, w_mod, b_mod):
    tn = 2304
    n_out = 9 * D_MODEL
    return pl.pallas_call(
        _mod_kernel,
        grid=(DEPTH, n_out // tn),
        in_specs=[
            pl.BlockSpec((MOD_ROWS, D_MODEL), lambda l, j: (0, 0)),
            pl.BlockSpec((None, D_MODEL, tn), lambda l, j: (l, 0, j)),
            pl.BlockSpec((None, 1, tn), lambda l, j: (l, 0, j)),
        ],
        out_specs=pl.BlockSpec((None, MOD_ROWS, tn), lambda l, j: (l, 0, j)),
        out_shape=jax.ShapeDtypeStruct((DEPTH, MOD_ROWS, n_out), F32),
        compiler_params=_cparams(("parallel", "parallel")),
        name="adaln_modulation",
    )(cc, w_mod, b_mod.reshape(DEPTH, 1, n_out))


def _token_tile(refs, rows, lat_tiles):
    if len(refs) == 1:
        return refs[0][rows, :]
    return jnp.where(pl.program_id(0) < lat_tiles, refs[0][rows, :], refs[1][rows, :])


def _ffn_kernel(*refs, n_x, n_y, sub, g_pre, g_post, lat_tiles):
    x_refs = refs[:n_x]
    m_ref, g_ref = refs[n_x:n_x + 2]
    y_refs = refs[n_x + 2:n_x + 2 + 4 * n_y]
    wo_ref = refs[n_x + 2 + 4 * n_y] if n_y else None
    wg_ref, wu_ref, wd_ref, o_ref = refs[-4:]
    pre = g_ref[g_pre:g_pre + 1, :] * (1.0 + m_ref[3 * sub + 1:3 * sub + 2, :])
    shift = m_ref[3 * sub:3 * sub + 1, :]
    post = (FFN_RESIDUAL * m_ref[3 * sub + 2:3 * sub + 3, :]) * g_ref[g_post:g_post + 1, :]
    mix_post = m_ref[5:6, :] * g_ref[3:4, :]
    rs = o_ref.shape[0] // FFN_SPLIT
    for s in range(FFN_SPLIT):
        rows = pl.ds(s * rs, rs)
        x = _token_tile(x_refs, rows, lat_tiles)
        if n_y:
            mixed = jnp.concatenate([_token_tile(y_refs[n_y * j:n_y * (j + 1)], rows, lat_tiles) for j in range(4)],
                                    axis=1)
            y = _dot(mixed.astype(F32), wo_ref[...])
            x = x + (y * _rms_rows(y)) * mix_post
        hm = x * _rms_rows(x) * pre + shift
        gate = _dot(hm, wg_ref[...])
        up = _dot(hm, wu_ref[...])
        y = _dot(_silu(gate) * up, wd_ref[...])
        o_ref[rows, :] = x + (y * _rms_rows(y)) * post


def _stream_specs(xs, tm, width):
    if len(xs) == 1:
        return [pl.BlockSpec((tm, width), lambda i: (i, 0))]
    lat_tiles = xs[0].shape[0] // tm
    return [pl.BlockSpec((tm, width), lambda i: (jnp.minimum(i, lat_tiles - 1), 0)),
            pl.BlockSpec((tm, width), lambda i: (jnp.maximum(i - lat_tiles, 0), 0))]


def _ffn_sublayer(xs, mod, gains, wg, wu, wd, *, layer, sub, rows, ys=(), w_out=None):
    tm = TM_FFN
    n_y = len(ys[0]) if ys else 0
    kern = functools.partial(_ffn_kernel, n_x=len(xs), n_y=n_y, sub=sub, g_pre=2 * sub, g_post=2 * sub + 1,
                             lat_tiles=T_LAT // tm)
    k = sub // 2
    resident = dict(pipeline_mode=pl.Buffered(1))
    y_specs, y_args = [], []
    for parts in ys:
        y_specs += _stream_specs(parts, tm, 256)
        y_args += list(parts)
    if ys:
        y_specs.append(pl.BlockSpec((None, D_MODEL, D_MODEL), lambda i: (layer, 0, 0), **resident))
        y_args.append(w_out)
    return pl.pallas_call(
        kern,
        grid=(rows // tm,),
        in_specs=_stream_specs(xs, tm, D_MODEL) + [
            pl.BlockSpec((None, 9, D_MODEL), lambda i: (_mod_row_index(i, tm), 0, 0)),
            pl.BlockSpec((6, D_MODEL), lambda i: (0, 0)),
        ] + y_specs + [
            pl.BlockSpec((None, None, D_MODEL, D_FF), lambda i: (layer, k, 0, 0), **resident),
            pl.BlockSpec((None, None, D_MODEL, D_FF), lambda i: (layer, k, 0, 0), **resident),
            pl.BlockSpec((None, None, D_FF, D_MODEL), lambda i: (layer, k, 0, 0), **resident),
        ],
        out_specs=pl.BlockSpec((tm, D_MODEL), lambda i: (i, 0)),
        out_shape=jax.ShapeDtypeStruct((rows, D_MODEL), F32),
        compiler_params=_cparams(("parallel",)),
        name=f"ffn_sublayer_{sub}",
    )(*xs, mod, gains, *y_args, wg, wu, wd)


def _rope(x, tab_ref, rows, quarter):
    w = x.shape[1]
    reps = w // tab_ref.shape[2]

    def tab(k):
        t = tab_ref[k, rows, :]
        return t if reps == 1 else jnp.concatenate([t] * reps, axis=1)

    return x * tab(0) + pltpu.roll(x, w - quarter, 1) * tab(1) + pltpu.roll(x, quarter, 1) * tab(2)


def _swap_middle_heads(x):
    first = lax.broadcasted_iota(jnp.int32, (1, 128), 1) < HEAD_DIM
    lo, hi = x[:, 0:128], x[:, 128:256]
    return jnp.concatenate([jnp.where(first, lo, pltpu.roll(hi, HEAD_DIM, 1)),
                            jnp.where(first, pltpu.roll(lo, HEAD_DIM, 1), hi)], axis=1)


def _inproj_kernel(x_ref, m_ref, g_ref, w_ref, ta_ref, tb_ref, qkg_ref,
                   qa_ref, ka_ref, va_ref, qb_ref, kb_ref, vb_ref, ucd_ref):
    pre = g_ref[2:3, :] * (1.0 + m_ref[4:5, :])
    shift = m_ref[3:4, :]
    ones = _block_ones(256, HEAD_DIM).astype(F32)
    inv_d = 1.0 / HEAD_DIM
    lane128 = lax.broadcasted_iota(jnp.int32, (1, 128), 1)
    rs = x_ref.shape[0] // PROJ_SPLIT
    for s in range(PROJ_SPLIT):
        rows = pl.ds(s * rs, rs)
        x = x_ref[rows, :]
        hm = x * _rms_rows(x) * pre + shift

        def proj(lo, hi):
            return _dot(hm, w_ref[:, lo:hi])

        q = proj(0, 256)
        q = q * lax.rsqrt(_group_sums(q * q, ones) * inv_d + EPS) * qkg_ref[0:1, :]
        q = _rope(q, ta_ref, rows, HEAD_DIM // 4) * (HEAD_DIM ** -0.5 * LOG2E)
        qa_ref[rows, :] = _swap_middle_heads(q).astype(BF16)
        k = proj(256, 384)
        k = k * lax.rsqrt(_group_sums(k * k, ones[:128, :128]) * inv_d + EPS) * qkg_ref[1:2, :128]
        ka_ref[rows, :] = _rope(k, ta_ref, rows, HEAD_DIM // 4).astype(BF16)
        v = proj(384, 512)
        for g in range(2):
            va_ref[rows, 128 * g:128 * g + 128] = jnp.where(lane128 == HEAD_DIM * (1 - g), 1.0, v).astype(BF16)

        qb_ref[rows, :] = (_rope(proj(512, 768), tb_ref, rows, DIFF_QK_DIM // 4)
                           * (DIFF_QK_DIM ** -0.5 * LOG2E)).astype(BF16)
        kb_ref[rows, :] = _rope(proj(768, 1024), tb_ref, rows, DIFF_QK_DIM // 4).astype(BF16)
        v = proj(1024, 1280)
        for hd in range(4):
            own = (lane128 // HEAD_DIM) == hd % 2
            ones_col = jnp.where(lane128 == HEAD_DIM * (1 - hd % 2), 1.0, 0.0)
            pair = v[:, 128 * (hd // 2):128 * (hd // 2) + 128]
            vb_ref[rows, 128 * hd:128 * hd + 128] = jnp.where(own, pair, ones_col).astype(BF16)

        ucd_ref[rows, :] = proj(1280, 2560)


def _in_projection(h, mod, gains, w_in, tab_a, tab_b, qk_gain_lanes, *, layer):
    tm = TM_PROJ
    lat_tiles = T_LAT // tm
    tiles_per_seq = SEQ // tm

    def tab_idx(i):
        return jnp.where(i < lat_tiles, i % tiles_per_seq, tiles_per_seq)

    def row_spec(width):
        return pl.BlockSpec((tm, width), lambda i: (i, 0))

    def out(width, dtype):
        return jax.ShapeDtypeStruct((T_ALL, width), dtype)

    return pl.pallas_call(
        _inproj_kernel,
        grid=(T_ALL // tm,),
        in_specs=[
            row_spec(D_MODEL),
            pl.BlockSpec((None, 9, D_MODEL), lambda i: (_mod_row_index(i, tm), 0, 0)),
            pl.BlockSpec((6, D_MODEL), lambda i: (0, 0)),
            pl.BlockSpec((None, D_MODEL, D_IN), lambda i: (layer, 0, 0), pipeline_mode=pl.Buffered(1)),
            pl.BlockSpec((3, tm, 128), lambda i: (0, tab_idx(i), 0)),
            pl.BlockSpec((3, tm, 128), lambda i: (0, tab_idx(i), 0)),
            pl.BlockSpec((2, 256), lambda i: (0, 0)),
        ],
        out_specs=[row_spec(256), row_spec(128), row_spec(256), row_spec(256), row_spec(256), row_spec(512),
                   row_spec(1280)],
        out_shape=[out(256, BF16), out(128, BF16), out(256, BF16), out(256, BF16), out(256, BF16), out(512, BF16),
                   out(1280, F32)],
        compiler_params=_cparams(("parallel",)),
        name="in_projection",
    )(h, mod, gains, w_in, tab_a, tab_b, qk_gain_lanes)


def _softmax_pv(qs, k_refs, v_refs, lanes, ones_lane):
    return _prob_values(_probabilities(qs, k_refs), v_refs, lanes, ones_lane)


def _probabilities(qs, k_refs):
    ss = [_dot_nt(qs, k[...]) for k in k_refs]
    m = ss[0].max(axis=-1, keepdims=True)
    for s in ss[1:]:
        m = jnp.maximum(m, s.max(axis=-1, keepdims=True))
    return [jnp.exp2(s - m).astype(BF16) for s in ss]


def _prob_values(ps, v_refs, lanes, ones_lane):
    half = ps[0].shape[0] // 2
    outs = []
    for r0 in (0, half):
        o = None
        for p, v in zip(ps, v_refs):
            pv = _dot(p[r0:r0 + half], v[:, lanes])
            o = pv if o is None else o + pv
        outs.append(o / o[:, ones_lane:ones_lane + 1])
    return outs


def _attn_kernel(*refs, n_parts, lam_init):
    qa_ref = refs[0]
    ka = refs[1:1 + n_parts]
    va = refs[1 + n_parts:1 + 2 * n_parts]
    qb_ref = refs[1 + 2 * n_parts]
    kb = refs[2 + 2 * n_parts:2 + 3 * n_parts]
    vb = refs[2 + 3 * n_parts:2 + 4 * n_parts]
    lam_ref, dg_ref, ya_ref, yb_ref = refs[2 + 4 * n_parts:]

    lane128 = lax.broadcasted_iota(jnp.int32, (1, 128), 1)
    first = lane128 < HEAD_DIM
    qa = qa_ref[...]
    zero = jnp.zeros((), BF16)
    outs = []
    for g in range(2):
        sel = (lane128 // HEAD_DIM) == g
        qs = jnp.concatenate([jnp.where(sel, qa[:, 0:128], zero), jnp.where(sel, qa[:, 128:256], zero)], axis=0)
        outs.append(_softmax_pv(qs, ka, va, slice(128 * g, 128 * g + 128), HEAD_DIM * (1 - g)))
    y0 = jnp.where(first, outs[0][0], outs[1][0])
    y1 = jnp.where(first, outs[0][1], outs[1][1])
    ya_ref[...] = _swap_middle_heads(jnp.concatenate([y0, y1], axis=1)).astype(BF16)

    lp = lam_ref[...]
    lam = (jnp.exp(jnp.sum(lp[0:1] * lp[1:2], axis=-1, keepdims=True))
           - jnp.exp(jnp.sum(lp[2:3] * lp[3:4], axis=-1, keepdims=True)) + lam_init)
    half = lax.broadcasted_iota(jnp.int32, (1, 256), 1) // DIFF_QK_DIM
    qb = qb_ref[...]
    d = []
    for h in range(4):
        qs = jnp.concatenate([jnp.where(half == 2 * h, qb, zero), jnp.where(half == 2 * h + 1, qb, zero)], axis=0)
        o1, o2 = _softmax_pv(qs, kb, vb, slice(128 * h, 128 * h + 128), HEAD_DIM * (1 - h % 2))
        d.append(o1 - lam * o2)
    y = jnp.concatenate([jnp.where(first, d[0], d[1]), jnp.where(first, d[2], d[3])], axis=1)
    ones = _block_ones(256, HEAD_DIM).astype(F32)
    yn = y * lax.rsqrt(_group_sums(y * y, ones) * (1.0 / HEAD_DIM) + EPS) * dg_ref[...]
    yb_ref[...] = (yn * (1.0 - lam_init)).astype(BF16)


def _attention(qa, ka, va, qb, kb, vb, diff_lambda, diff_gain_lanes, *, lam_init, latent):
    ctx_blk0 = T_LAT // CTX_LEN
    if latent:
        grid = (BATCH, SEQ // TQ)
        q_map = lambda b, i: (b * (SEQ // TQ) + i, 0)
        kv_specs = lambda w: [pl.BlockSpec((CTX_LEN, w), lambda b, i: (ctx_blk0 + b, 0)),
                              pl.BlockSpec((SEQ, w), lambda b, i: (b, 0))]
        rows = T_LAT
        n_parts = 2
        const = lambda b, i: (0, 0)
        o_map = q_map
    else:
        grid = (BATCH,)
        q_map = lambda b: (ctx_blk0 + b, 0)
        kv_specs = lambda w: [pl.BlockSpec((CTX_LEN, w), lambda b: (ctx_blk0 + b, 0))]
        rows = T_CTX
        n_parts = 1
        const = lambda b: (0, 0)
        o_map = lambda b: (b, 0)
    in_specs = ([pl.BlockSpec((TQ, 256), q_map)] + kv_specs(128) + kv_specs(256)
                + [pl.BlockSpec((TQ, 256), q_map)] + kv_specs(256) + kv_specs(512)
                + [pl.BlockSpec((4, DIFF_QK_DIM), const), pl.BlockSpec((1, 256), const)])
    args = [qa] + [ka] * n_parts + [va] * n_parts + [qb] + [kb] * n_parts + [vb] * n_parts
    return pl.pallas_call(
        functools.partial(_attn_kernel, n_parts=n_parts, lam_init=lam_init),
        grid=grid,
        in_specs=in_specs,
        out_specs=[pl.BlockSpec((TQ, 256), o_map), pl.BlockSpec((TQ, 256), o_map)],
        out_shape=[jax.ShapeDtypeStruct((rows, 256), BF16), jax.ShapeDtypeStruct((rows, 256), BF16)],
        compiler_params=_cparams(("parallel",) * len(grid)),
        name="attention_latent" if latent else "attention_context",
    )(*args, diff_lambda, diff_gain_lanes)


POOL_PAD = 16


def _pool(u_ref, n_seq, pw, ps, pa, pb, pc):
    pad = POOL_PAD
    n = n_seq + pad
    z = jnp.zeros((pad, 128), F32)
    first = lax.broadcasted_iota(jnp.int32, (1, 128), 1) < 64

    def shifted_sum(src, dst, k):
        dst[0:n, :] = src[0:n, :] + src[k:n + k, :]

    tiles = []
    for tile in range(2):
        pa[0:pad, :] = z
        pa[pad:n, :] = u_ref[0:n_seq, 128 * tile:128 * tile + 128]
        pa[n:n + pad, :] = z
        pb[n:n + pad, :] = z
        pc[n:n + pad, :] = z
        shifted_sum(pa, pb, 1)
        shifted_sum(pb, pc, 2)
        if tile == 0:
            w_lo = pb[pad - 1:pad - 1 + n_seq, :]
            w_hi = pc[pad - 2:pad - 2 + n_seq, :]
        else:
            shifted_sum(pc, pb, 4)
            w_lo = pb[pad - 4:pad - 4 + n_seq, :]
            shifted_sum(pb, pc, 8)
            w_hi = pc[pad - 8:pad - 8 + n_seq, :]
        tiles.append(jnp.where(first, w_lo, w_hi))
    wsum = jnp.concatenate(tiles, axis=1)
    grp = lax.broadcasted_iota(jnp.int32, (1, 256), 1) // 64
    back = jnp.where(grp == 0, 1, jnp.where(grp == 1, 2, jnp.where(grp == 2, 4, 8)))
    t = lax.broadcasted_iota(jnp.int32, (n_seq, 256), 0)
    count = jnp.minimum(t + (back - 1), n_seq - 1) - jnp.maximum(t - back, 0) + 1
    pooled = wsum / count.astype(F32) - u_ref[0:n_seq, 0:256]
    return _dot(pooled, pw) * ps


def _log_sigmoid(x):
    return jnp.minimum(x, 0.0) - jnp.log(1.0 + jnp.exp(-jnp.abs(x)))


def _mix_kernel(*refs, with_ctx):
    if with_ctx:
        (ul_ref, uc_ref, dl_ref, dh_ref, rg_ref, pw_ref, ps_ref,
         ycl_ref, ydl_ref, ycc_ref, ydc_ref, tst_ref, pa, pb, pc) = refs
    else:
        (ul_ref, uc_ref, dl_ref, dh_ref, rg_ref, pw_ref, ps_ref, ycl_ref, ydl_ref, tst_ref, pa, pb, pc) = refs
    c = RET_C
    n_chunks = SEQ // c
    pw = pw_ref[...]
    ps = ps_ref[...]

    ycl_ref[...] = _pool(ul_ref, SEQ, pw, ps, pa, pb, pc).astype(BF16)
    if with_ctx:
        ycc_ref[...] = _pool(uc_ref, CTX_LEN, pw, ps, pa, pb, pc).astype(BF16)

    lg = _log_sigmoid(dl_ref[...])
    lgh = _log_sigmoid(dh_ref[...])
    lgf, lgb = lg[0:1, :], lg[1:2, :]
    pos = lax.broadcasted_iota(jnp.int32, (c, 1), 0).astype(F32)
    wkf = jnp.exp((c - 1.0 - pos) * lgf)
    wkb = jnp.exp(pos * lgb)
    wqf = jnp.exp((pos + 1.0) * lgf)
    wqb = jnp.exp((c - pos) * lgb)
    dcf = jnp.exp(c * lgf)
    dcb = jnp.exp(c * lgb)
    same_head = _block_ones(256, HEAD_DIM)
    ones = same_head.astype(F32)
    lane = lax.broadcasted_iota(jnp.int32, (1, 256), 1) // HEAD_DIM
    diff = (lax.broadcasted_iota(jnp.int32, (c, c), 0) - lax.broadcasted_iota(jnp.int32, (c, c), 1)).astype(F32)
    decay = []
    for h in range(4):
        df = jnp.exp(jnp.maximum(diff, 0.0) * lgh[h:h + 1, 0:1])
        db = jnp.exp(jnp.maximum(-diff, 0.0) * lgh[4 + h:5 + h, 0:1])
        decay.append(jnp.where(diff > 0, df, jnp.where(diff < 0, db, 2.0)))
    rg = rg_ref[...]
    k_scale = HEAD_DIM ** -0.5

    def load(ref, r0):
        rows = pl.ds(r0, c)
        return ref[rows, 256:512], ref[rows, 512:768] * k_scale, ref[rows, 768:1024], ref[rows, 1024:1280]

    def increment(k, v, wk):
        return jnp.where(same_head, _dot_tn(k * wk, v), 0.0)

    def chunk_out(q, k, v, gate, s_fwd, s_bwd):
        o = _dot(q * wqf, s_fwd) + _dot(q * wqb, s_bwd)
        for h in range(4):
            sel = lane == h
            s = _dot_nt(jnp.where(sel, q, 0.0), k)
            o = o + _dot(s * decay[h], jnp.where(sel, v, 0.0))
        on = o * lax.rsqrt(_group_sums(o * o, ones) * (1.0 / HEAD_DIM) + EPS) * rg
        return (on * _silu(gate)).astype(BF16)

    _, kc, vc, _ = load(uc_ref, 0)
    s_fwd = increment(kc, vc, wkf)
    s_bwd = increment(kc, vc, wkb)

    for j in range(n_chunks - 1, -1, -1):
        tst_ref[j] = s_bwd
        if j > 0:
            _, k, v, _ = load(ul_ref, j * c)
            s_bwd = s_bwd * dcb + increment(k, v, wkb)

    for j in range(n_chunks):
        q, k, v, gate = load(ul_ref, j * c)
        ydl_ref[pl.ds(j * c, c), :] = chunk_out(q, k, v, gate, s_fwd, tst_ref[j])
        if j < n_chunks - 1:
            s_fwd = s_fwd * dcf + increment(k, v, wkf)

    if with_ctx:
        qc, kc, vc, gc = load(uc_ref, 0)
        zeros = jnp.zeros((256, 256), F32)
        ydc_ref[...] = chunk_out(qc, kc, vc, gc, zeros, zeros)


def _mixer(ucd, decay_lanes, decay_heads, ret_gain_lanes, pool_w_bd, pool_scale, *, with_ctx):
    ctx_blk0 = T_LAT // CTX_LEN
    const = lambda b: (0, 0)
    out_specs = [pl.BlockSpec((SEQ, 256), lambda b: (b, 0))] * 2
    out_shape = [jax.ShapeDtypeStruct((T_LAT, 256), BF16)] * 2
    if with_ctx:
        out_specs = out_specs + [pl.BlockSpec((CTX_LEN, 256), lambda b: (b, 0))] * 2
        out_shape = out_shape + [jax.ShapeDtypeStruct((T_CTX, 256), BF16)] * 2
    return pl.pallas_call(
        functools.partial(_mix_kernel, with_ctx=with_ctx),
        grid=(BATCH,),
        in_specs=[
            pl.BlockSpec((SEQ, 1280), lambda b: (b, 0)),
            pl.BlockSpec((CTX_LEN, 1280), lambda b: (ctx_blk0 + b, 0)),
            pl.BlockSpec((2, 256), const),
            pl.BlockSpec((8, 128), const),
            pl.BlockSpec((1, 256), const),
            pl.BlockSpec((256, 256), const),
            pl.BlockSpec((1, 256), const),
        ],
        out_specs=out_specs,
        out_shape=out_shape,
        scratch_shapes=[pltpu.VMEM((SEQ // RET_C, 256, 256), F32)]
        + [pltpu.VMEM((SEQ + 2 * POOL_PAD, 128), F32)] * 3,
        compiler_params=_cparams(("parallel",)),
        name="pool_retention_mixer",
    )(ucd, ucd, decay_lanes, decay_heads, ret_gain_lanes, pool_w_bd, pool_scale)


def _rope_tables(dim, tm):
    q = dim // 4
    n_rows = SEQ // GRID_W
    rows = np.repeat(np.arange(n_rows), GRID_W).astype(np.float64)
    cols = np.tile(np.arange(GRID_W), n_rows).astype(np.float64)
    inv = ROPE_BASE ** (-np.arange(q, dtype=np.float64) / q)
    ar, ac = rows[:, None] * inv, cols[:, None] * inv
    cos = np.concatenate([np.cos(ar)] * 2 + [np.cos(ac)] * 2, axis=-1)
    sin = np.concatenate([np.sin(ar)] * 2 + [np.sin(ac)] * 2, axis=-1)
    first_half = np.tile(np.repeat(np.array([True, False]), q), 2)
    tabs = np.stack([cos, np.where(first_half, -sin, 0.0), np.where(first_half, 0.0, sin)])
    tabs = np.tile(tabs, (1, 1, 128 // dim))
    ident = np.stack([np.ones((tm, 128)), np.zeros((tm, 128)), np.zeros((tm, 128))])
    return jnp.asarray(np.concatenate([tabs, ident], axis=1), dtype=F32)


def _block_diag(blocks):
    n, d, _ = blocks.shape
    eye = jnp.eye(n, dtype=blocks.dtype)
    return jnp.einsum('gcd,gh->gchd', blocks, eye).reshape(n * d, n * d)


def kernel(x, c, ctx, c_ctx, w_mod, b_mod, norm_gain, ffn_w_gate, ffn_w_up, ffn_w_down, w_in, w_out,
           attn_qk_gain, diff_lambda, diff_out_gain, pool_w, pool_scale, ret_decay_logit, ret_out_gain):
    xs = (x.reshape(T_LAT, D_MODEL), ctx.reshape(T_CTX, D_MODEL))
    cc = jnp.concatenate([c, c_ctx[None, :], jnp.zeros((MOD_ROWS - BATCH - 1, D_MODEL), F32)], axis=0)
    mod_all = _modulation(cc, w_mod, b_mod).reshape(DEPTH, MOD_ROWS, 9, D_MODEL)
    tab_a = _rope_tables(HEAD_DIM, TM_PROJ)
    tab_b = _rope_tables(DIFF_QK_DIM, TM_PROJ)

    for i in range(DEPTH):
        last = i == DEPTH - 1
        mod = mod_all[i]
        gains = norm_gain[i]
        qk_gain_lanes = jnp.tile(attn_qk_gain[i], (1, 256 // HEAD_DIM))
        diff_gain_lanes = jnp.tile(diff_out_gain[i], 256 // HEAD_DIM)[None, :]
        ret_gain_lanes = jnp.tile(ret_out_gain[i], 256 // HEAD_DIM)[None, :]
        decay_lanes = jnp.repeat(ret_decay_logit[i], HEAD_DIM, axis=1)
        decay_heads = jnp.broadcast_to(ret_decay_logit[i].reshape(8, 1), (8, 128))
        pool_w_bd = _block_diag(pool_w[i])
        lam_init = 0.8 - 0.6 * math.exp(-0.3 * i)

        def ffn(hs, sub, rows, **mix):
            return _ffn_sublayer(hs, mod, gains, ffn_w_gate, ffn_w_up, ffn_w_down, layer=i, sub=sub, rows=rows,
                                 **mix)

        h = ffn(xs, 0, T_ALL)
        qa, ka, va, qb, kb, vb, ucd = _in_projection(h, mod, gains, w_in, tab_a, tab_b, qk_gain_lanes, layer=i)
        ya, yb = _attention(qa, ka, va, qb, kb, vb, diff_lambda[i], diff_gain_lanes, lam_init=lam_init, latent=True)
        mixed = _mixer(ucd, decay_lanes, decay_heads, ret_gain_lanes, pool_w_bd, pool_scale[i][None, :],
                       with_ctx=not last)
        if last:
            ys, rows = [(ya,), (yb,), (mixed[0],), (mixed[1],)], T_LAT
        else:
            ya_c, yb_c = _attention(qa, ka, va, qb, kb, vb, diff_lambda[i], diff_gain_lanes, lam_init=lam_init,
                                    latent=False)
            ys, rows = [(ya, ya_c), (yb, yb_c), (mixed[0], mixed[2]), (mixed[1], mixed[3])], T_ALL
        h = ffn((h,), 2, rows, ys=ys, w_out=w_out)
        xs = (h,)
    return h.reshape(BATCH, SEQ, D_MODEL)
```

```python
import functools
import math

import jax
import jax.numpy as jnp
import numpy as np
from jax import lax
from jax.experimental import pallas as pl
from jax.experimental.pallas import tpu as pltpu

F32 = jnp.float32
BF16 = jnp.bfloat16

D_MODEL = 1024
BATCH = 8
SEQ = 2048
DEPTH = 2
CTX_LEN = 256
GRID_W = 64
HEAD_DIM = 64
DIFF_QK_DIM = 32
GROUP_WIDTH = 256
D_FF = 2816
D_IN = 2560
FFN_RESIDUAL = 0.5
ROPE_BASE = 10000.0
EPS = 1e-6

T_LAT = BATCH * SEQ
T_CTX = BATCH * CTX_LEN
T_ALL = T_LAT + T_CTX
MOD_ROWS = 16

TM_FFN = 512
FFN_SPLIT = 2
TM_PROJ = 1024
PROJ_SPLIT = 4
LOG2E = 1.4426950408889634
TQ = 256
RET_C = 256
VMEM_LIMIT = 56 * 1024 * 1024


def _cparams(sem):
    return pltpu.CompilerParams(dimension_semantics=sem, vmem_limit_bytes=VMEM_LIMIT)


def _silu(x):
    return x * jax.nn.sigmoid(x)


_MXU = dict(preferred_element_type=F32, precision=lax.Precision.DEFAULT)


def _dot(a, b):
    return jnp.dot(a, b, **_MXU)


def _dot_nt(a, b):
    return lax.dot_general(a, b, (((1,), (1,)), ((), ())), **_MXU)


def _dot_tn(a, b):
    return lax.dot_general(a, b, (((0,), (0,)), ((), ())), **_MXU)


def _block_ones(n, blk):
    r = lax.broadcasted_iota(jnp.int32, (n, n), 0) // blk
    c = lax.broadcasted_iota(jnp.int32, (n, n), 1) // blk
    return r == c


def _group_sums(sq, ones):
    return _dot(sq, ones)


def _rms_rows(x):
    return lax.rsqrt(jnp.mean(x * x, axis=-1, keepdims=True) + EPS)


def _mod_row_index(tile, tm):
    return jnp.minimum((tile * tm) // SEQ, BATCH)


def _mod_kernel(cc_ref, w_ref, b_ref, o_ref):
    o_ref[...] = _dot(_silu(cc_ref[...]), w_ref[...]) + b_ref[...]


def _modulation(cc, w_mod, b_mod):
    tn = 2304
    n_out = 9 * D_MODEL
    return pl.pallas_call(
        _mod_kernel,
        grid=(DEPTH, n_out // tn),
        in_specs=[
            pl.BlockSpec((MOD_ROWS, D_MODEL), lambda l, j: (0, 0)),
            pl.BlockSpec((None, D_MODEL, tn), lambda l, j: (l, 0, j)),
            pl.BlockSpec((None, 1, tn), lambda l, j: (l, 0, j)),
        ],
        out_specs=pl.BlockSpec((None, MOD_ROWS, tn), lambda l, j: (l, 0, j)),
        out_shape=jax.ShapeDtypeStruct((DEPTH, MOD_ROWS, n_out), F32),
        compiler_params=_cparams(("parallel", "parallel")),
        name="adaln_modulation",
    )(cc, w_mod, b_mod.reshape(DEPTH, 1, n_out))


def _token_tile(refs, rows, lat_tiles):
    if len(refs) == 1:
        return refs[0][rows, :]
    return jnp.where(pl.program_id(0) < lat_tiles, refs[0][rows, :], refs[1][rows, :])


def _ffn_kernel(*refs, n_x, n_y, sub, g_pre, g_post, lat_tiles):
    x_refs = refs[:n_x]
    m_ref, g_ref = refs[n_x:n_x + 2]
    y_refs = refs[n_x + 2:n_x + 2 + 4 * n_y]
    wo_ref = refs[n_x + 2 + 4 * n_y] if n_y else None
    wg_ref, wu_ref, wd_ref, o_ref = refs[-4:]
    pre = g_ref[g_pre:g_pre + 1, :] * (1.0 + m_ref[3 * sub + 1:3 * sub + 2, :])
    shift = m_ref[3 * sub:3 * sub + 1, :]
    post = (FFN_RESIDUAL * m_ref[3 * sub + 2:3 * sub + 3, :]) * g_ref[g_post:g_post + 1, :]
    mix_post = m_ref[5:6, :] * g_ref[3:4, :]
    rs = o_ref.shape[0] // FFN_SPLIT
    for s in range(FFN_SPLIT):
        rows = pl.ds(s * rs, rs)
        x = _token_tile(x_refs, rows, lat_tiles)
        if n_y:
            mixed = jnp.concatenate([_token_tile(y_refs[n_y * j:n_y * (j + 1)], rows, lat_tiles) for j in range(4)],
                                    axis=1)
            y = _dot(mixed.astype(F32), wo_ref[...])
            x = x + (y * _rms_rows(y)) * mix_post
        hm = x * _rms_rows(x) * pre + shift
        gate = _dot(hm, wg_ref[...])
        up = _dot(hm, wu_ref[...])
        y = _dot(_silu(gate) * up, wd_ref[...])
        o_ref[rows, :] = x + (y * _rms_rows(y)) * post


def _stream_specs(xs, tm, width):
    if len(xs) == 1:
        return [pl.BlockSpec((tm, width), lambda i: (i, 0))]
    lat_tiles = xs[0].shape[0] // tm
    return [pl.BlockSpec((tm, width), lambda i: (jnp.minimum(i, lat_tiles - 1), 0)),
            pl.BlockSpec((tm, width), lambda i: (jnp.maximum(i - lat_tiles, 0), 0))]


def _ffn_sublayer(xs, mod, gains, wg, wu, wd, *, layer, sub, rows, ys=(), w_out=None):
    tm = TM_FFN
    n_y = len(ys[0]) if ys else 0
    kern = functools.partial(_ffn_kernel, n_x=len(xs), n_y=n_y, sub=sub, g_pre=2 * sub, g_post=2 * sub + 1,
                             lat_tiles=T_LAT // tm)
    k = sub // 2
    resident = dict(pipeline_mode=pl.Buffered(1))
    y_specs, y_args = [], []
    for parts in ys:
        y_specs += _stream_specs(parts, tm, 256)
        y_args += list(parts)
    if ys:
        y_specs.append(pl.BlockSpec((None, D_MODEL, D_MODEL), lambda i: (layer, 0, 0), **resident))
        y_args.append(w_out)
    return pl.pallas_call(
        kern,
        grid=(rows // tm,),
        in_specs=_stream_specs(xs, tm, D_MODEL) + [
            pl.BlockSpec((None, 9, D_MODEL), lambda i: (_mod_row_index(i, tm), 0, 0)),
            pl.BlockSpec((6, D_MODEL), lambda i: (0, 0)),
        ] + y_specs + [
            pl.BlockSpec((None, None, D_MODEL, D_FF), lambda i: (layer, k, 0, 0), **resident),
            pl.BlockSpec((None, None, D_MODEL, D_FF), lambda i: (layer, k, 0, 0), **resident),
            pl.BlockSpec((None, None, D_FF, D_MODEL), lambda i: (layer, k, 0, 0), **resident),
        ],
        out_specs=pl.BlockSpec((tm, D_MODEL), lambda i: (i, 0)),
        out_shape=jax.ShapeDtypeStruct((rows, D_MODEL), F32),
        compiler_params=_cparams(("parallel",)),
        name=f"ffn_sublayer_{sub}",
    )(*xs, mod, gains, *y_args, wg, wu, wd)


def _rope(x, tab_ref, rows, quarter):
    w = x.shape[1]
    reps = w // tab_ref.shape[2]

    def tab(k):
        t = tab_ref[k, rows, :]
        return t if reps == 1 else jnp.concatenate([t] * reps, axis=1)

    return x * tab(0) + pltpu.roll(x, w - quarter, 1) * tab(1) + pltpu.roll(x, quarter, 1) * tab(2)


def _swap_middle_heads(x):
    first = lax.broadcasted_iota(jnp.int32, (1, 128), 1) < HEAD_DIM
    lo, hi = x[:, 0:128], x[:, 128:256]
    return jnp.concatenate([jnp.where(first, lo, pltpu.roll(hi, HEAD_DIM, 1)),
                            jnp.where(first, pltpu.roll(lo, HEAD_DIM, 1), hi)], axis=1)


def _inproj_kernel(x_ref, m_ref, g_ref, w_ref, ta_ref, tb_ref, qkg_ref,
                   qa_ref, ka_ref, va_ref, qb_ref, kb_ref, vb_ref, ucd_ref):
    pre = g_ref[2:3, :] * (1.0 + m_ref[4:5, :])
    shift = m_ref[3:4, :]
    ones = _block_ones(256, HEAD_DIM).astype(F32)
    inv_d = 1.0 / HEAD_DIM
    lane128 = lax.broadcasted_iota(jnp.int32, (1, 128), 1)
    rs = x_ref.shape[0] // PROJ_SPLIT
    for s in range(PROJ_SPLIT):
        rows = pl.ds(s * rs, rs)
        x = x_ref[rows, :]
        hm = x * _rms_rows(x) * pre + shift

        def proj(lo, hi):
            return _dot(hm, w_ref[:, lo:hi])

        q = proj(0, 256)
        q = q * lax.rsqrt(_group_sums(q * q, ones) * inv_d + EPS) * qkg_ref[0:1, :]
        q = _rope(q, ta_ref, rows, HEAD_DIM // 4) * (HEAD_DIM ** -0.5 * LOG2E)
        qa_ref[rows, :] = _swap_middle_heads(q).astype(BF16)
        k = proj(256, 384)
        k = k * lax.rsqrt(_group_sums(k * k, ones[:128, :128]) * inv_d + EPS) * qkg_ref[1:2, :128]
        ka_ref[rows, :] = _rope(k, ta_ref, rows, HEAD_DIM // 4).astype(BF16)
        v = proj(384, 512)
        for g in range(2):
            va_ref[rows, 128 * g:128 * g + 128] = jnp.where(lane128 == HEAD_DIM * (1 - g), 1.0, v).astype(BF16)

        qb_ref[rows, :] = (_rope(proj(512, 768), tb_ref, rows, DIFF_QK_DIM // 4)
                           * (DIFF_QK_DIM ** -0.5 * LOG2E)).astype(BF16)
        kb_ref[rows, :] = _rope(proj(768, 1024), tb_ref, rows, DIFF_QK_DIM // 4).astype(BF16)
        v = proj(1024, 1280)
        for hd in range(4):
            own = (lane128 // HEAD_DIM) == hd % 2
            ones_col = jnp.where(lane128 == HEAD_DIM * (1 - hd % 2), 1.0, 0.0)
            pair = v[:, 128 * (hd // 2):128 * (hd // 2) + 128]
            vb_ref[rows, 128 * hd:128 * hd + 128] = jnp.where(own, pair, ones_col).astype(BF16)

        ucd_ref[rows, :] = proj(1280, 2560)


def _in_projection(h, mod, gains, w_in, tab_a, tab_b, qk_gain_lanes, *, layer):
    tm = TM_PROJ
    lat_tiles = T_LAT // tm
    tiles_per_seq = SEQ // tm

    def tab_idx(i):
        return jnp.where(i < lat_tiles, i % tiles_per_seq, tiles_per_seq)

    def row_spec(width):
        return pl.BlockSpec((tm, width), lambda i: (i, 0))

    def out(width, dtype):
        return jax.ShapeDtypeStruct((T_ALL, width), dtype)

    return pl.pallas_call(
        _inproj_kernel,
        grid=(T_ALL // tm,),
        in_specs=[
            row_spec(D_MODEL),
            pl.BlockSpec((None, 9, D_MODEL), lambda i: (_mod_row_index(i, tm), 0, 0)),
            pl.BlockSpec((6, D_MODEL), lambda i: (0, 0)),
            pl.BlockSpec((None, D_MODEL, D_IN), lambda i: (layer, 0, 0), pipeline_mode=pl.Buffered(1)),
            pl.BlockSpec((3, tm, 128), lambda i: (0, tab_idx(i), 0)),
            pl.BlockSpec((3, tm, 128), lambda i: (0, tab_idx(i), 0)),
            pl.BlockSpec((2, 256), lambda i: (0, 0)),
        ],
        out_specs=[row_spec(256), row_spec(128), row_spec(256), row_spec(256), row_spec(256), row_spec(512),
                   row_spec(1280)],
        out_shape=[out(256, BF16), out(128, BF16), out(256, BF16), out(256, BF16), out(256, BF16), out(512, BF16),
                   out(1280, F32)],
        compiler_params=_cparams(("parallel",)),
        name="in_projection",
    )(h, mod, gains, w_in, tab_a, tab_b, qk_gain_lanes)


def _softmax_pv(q_blocks, k_refs, v_refs, lanes_of, ones_lane_of):
    tq = q_blocks[0].shape[0]
    qs = jnp.concatenate(q_blocks, axis=0)
    ss = [_dot_nt(qs, k[...]) for k in k_refs]
    m = ss[0].max(axis=-1, keepdims=True)
    for s in ss[1:]:
        m = jnp.maximum(m, s.max(axis=-1, keepdims=True))
    ps = [jnp.exp2(s - m).astype(BF16) for s in ss]
    outs = []
    for b in range(len(q_blocks)):
        o = None
        for p, v in zip(ps, v_refs):
            pv = _dot(p[tq * b:tq * (b + 1)], v[:, lanes_of(b)])
            o = pv if o is None else o + pv
        lane = ones_lane_of(b)
        outs.append(o / o[:, lane:lane + 1])
    return outs


def _attn_kernel(*refs, n_parts, lam_init):
    qa_ref = refs[0]
    ka = refs[1:1 + n_parts]
    va = refs[1 + n_parts:1 + 2 * n_parts]
    qb_ref = refs[1 + 2 * n_parts]
    kb = refs[2 + 2 * n_parts:2 + 3 * n_parts]
    vb = refs[2 + 3 * n_parts:2 + 4 * n_parts]
    lam_ref, dg_ref, ya_ref, yb_ref = refs[2 + 4 * n_parts:]

    lane128 = lax.broadcasted_iota(jnp.int32, (1, 128), 1)
    first = lane128 < HEAD_DIM
    qa = qa_ref[...]
    zero = jnp.zeros((), BF16)
    blocks = [jnp.where((lane128 // HEAD_DIM) == g, qa[:, 128 * j:128 * j + 128], zero)
              for g in range(2) for j in range(2)]
    outs = _softmax_pv(blocks, ka, va, lambda b: slice(128 * (b // 2), 128 * (b // 2) + 128),
                       lambda b: HEAD_DIM * (1 - b // 2))
    y0 = jnp.where(first, outs[0], outs[2])
    y1 = jnp.where(first, outs[1], outs[3])
    ya_ref[...] = _swap_middle_heads(jnp.concatenate([y0, y1], axis=1)).astype(BF16)

    lp = lam_ref[...]
    lam = (jnp.exp(jnp.sum(lp[0:1] * lp[1:2], axis=-1, keepdims=True))
           - jnp.exp(jnp.sum(lp[2:3] * lp[3:4], axis=-1, keepdims=True)) + lam_init)
    half = lax.broadcasted_iota(jnp.int32, (1, 256), 1) // DIFF_QK_DIM
    qb = qb_ref[...]
    blocks = [jnp.where(half == b, qb, zero) for b in range(8)]
    outs = _softmax_pv(blocks, kb, vb, lambda b: slice(128 * (b // 2), 128 * (b // 2) + 128),
                       lambda b: HEAD_DIM * (1 - (b // 2) % 2))
    d = [outs[2 * h] - lam * outs[2 * h + 1] for h in range(4)]
    y = jnp.concatenate([jnp.where(first, d[0], d[1]), jnp.where(first, d[2], d[3])], axis=1)
    ones = _block_ones(256, HEAD_DIM).astype(F32)
    yn = y * lax.rsqrt(_group_sums(y * y, ones) * (1.0 / HEAD_DIM) + EPS) * dg_ref[...]
    yb_ref[...] = (yn * (1.0 - lam_init)).astype(BF16)


def _attention(qa, ka, va, qb, kb, vb, diff_lambda, diff_gain_lanes, *, lam_init, latent):
    ctx_blk0 = T_LAT // CTX_LEN
    if latent:
        grid = (BATCH, SEQ // TQ)
        q_map = lambda b, i: (b * (SEQ // TQ) + i, 0)
        kv_specs = lambda w: [pl.BlockSpec((CTX_LEN, w), lambda b, i: (ctx_blk0 + b, 0)),
                              pl.BlockSpec((SEQ, w), lambda b, i: (b, 0))]
        rows = T_LAT
        n_parts = 2
        const = lambda b, i: (0, 0)
        o_map = q_map
    else:
        grid = (BATCH,)
        q_map = lambda b: (ctx_blk0 + b, 0)
        kv_specs = lambda w: [pl.BlockSpec((CTX_LEN, w), lambda b: (ctx_blk0 + b, 0))]
        rows = T_CTX
        n_parts = 1
        const = lambda b: (0, 0)
        o_map = lambda b: (b, 0)
    in_specs = ([pl.BlockSpec((TQ, 256), q_map)] + kv_specs(128) + kv_specs(256)
                + [pl.BlockSpec((TQ, 256), q_map)] + kv_specs(256) + kv_specs(512)
                + [pl.BlockSpec((4, DIFF_QK_DIM), const), pl.BlockSpec((1, 256), const)])
    args = [qa] + [ka] * n_parts + [va] * n_parts + [qb] + [kb] * n_parts + [vb] * n_parts
    return pl.pallas_call(
        functools.partial(_attn_kernel, n_parts=n_parts, lam_init=lam_init),
        grid=grid,
        in_specs=in_specs,
        out_specs=[pl.BlockSpec((TQ, 256), o_map), pl.BlockSpec((TQ, 256), o_map)],
        out_shape=[jax.ShapeDtypeStruct((rows, 256), BF16), jax.ShapeDtypeStruct((rows, 256), BF16)],
        compiler_params=_cparams(("parallel",) * len(grid)),
        name="attention_latent" if latent else "attention_context",
    )(*args, diff_lambda, diff_gain_lanes)


POOL_PAD = 16


def _pool(u_ref, n_seq, pw, ps, pa, pb, pc):
    pad = POOL_PAD
    n = n_seq + pad
    z = jnp.zeros((pad, 128), F32)
    first = lax.broadcasted_iota(jnp.int32, (1, 128), 1) < 64

    def shifted_sum(src, dst, k):
        dst[0:n, :] = src[0:n, :] + src[k:n + k, :]

    tiles = []
    for tile in range(2):
        pa[0:pad, :] = z
        pa[pad:n, :] = u_ref[0:n_seq, 128 * tile:128 * tile + 128]
        pa[n:n + pad, :] = z
        pb[n:n + pad, :] = z
        pc[n:n + pad, :] = z
        shifted_sum(pa, pb, 1)
        shifted_sum(pb, pc, 2)
        if tile == 0:
            w_lo = pb[pad - 1:pad - 1 + n_seq, :]
            w_hi = pc[pad - 2:pad - 2 + n_seq, :]
        else:
            shifted_sum(pc, pb, 4)
            w_lo = pb[pad - 4:pad - 4 + n_seq, :]
            shifted_sum(pb, pc, 8)
            w_hi = pc[pad - 8:pad - 8 + n_seq, :]
        tiles.append(jnp.where(first, w_lo, w_hi))
    wsum = jnp.concatenate(tiles, axis=1)
    grp = lax.broadcasted_iota(jnp.int32, (1, 256), 1) // 64
    back = jnp.where(grp == 0, 1, jnp.where(grp == 1, 2, jnp.where(grp == 2, 4, 8)))
    t = lax.broadcasted_iota(jnp.int32, (n_seq, 256), 0)
    count = jnp.minimum(t + (back - 1), n_seq - 1) - jnp.maximum(t - back, 0) + 1
    pooled = wsum / count.astype(F32) - u_ref[0:n_seq, 0:256]
    return _dot(pooled, pw) * ps


def _log_sigmoid(x):
    return jnp.minimum(x, 0.0) - jnp.log(1.0 + jnp.exp(-jnp.abs(x)))


def _mix_kernel(*refs, with_ctx):
    if with_ctx:
        (ul_ref, uc_ref, dl_ref, dh_ref, rg_ref, pw_ref, ps_ref,
         ycl_ref, ydl_ref, ycc_ref, ydc_ref, tst_ref, pa, pb, pc) = refs
    else:
        (ul_ref, uc_ref, dl_ref, dh_ref, rg_ref, pw_ref, ps_ref, ycl_ref, ydl_ref, tst_ref, pa, pb, pc) = refs
    c = RET_C
    n_chunks = SEQ // c
    pw = pw_ref[...]
    ps = ps_ref[...]

    ycl_ref[...] = _pool(ul_ref, SEQ, pw, ps, pa, pb, pc).astype(BF16)
    if with_ctx:
        ycc_ref[...] = _pool(uc_ref, CTX_LEN, pw, ps, pa, pb, pc).astype(BF16)

    lg = _log_sigmoid(dl_ref[...])
    lgh = _log_sigmoid(dh_ref[...])
    lgf, lgb = lg[0:1, :], lg[1:2, :]
    pos = lax.broadcasted_iota(jnp.int32, (c, 1), 0).astype(F32)
    wkf = jnp.exp((c - 1.0 - pos) * lgf)
    wkb = jnp.exp(pos * lgb)
    wqf = jnp.exp((pos + 1.0) * lgf)
    wqb = jnp.exp((c - pos) * lgb)
    dcf = jnp.exp(c * lgf)
    dcb = jnp.exp(c * lgb)
    same_head = _block_ones(256, HEAD_DIM)
    ones = same_head.astype(F32)
    lane = lax.broadcasted_iota(jnp.int32, (1, 256), 1) // HEAD_DIM
    diff = (lax.broadcasted_iota(jnp.int32, (c, c), 0) - lax.broadcasted_iota(jnp.int32, (c, c), 1)).astype(F32)
    decay = []
    for h in range(4):
        df = jnp.exp(jnp.maximum(diff, 0.0) * lgh[h:h + 1, 0:1])
        db = jnp.exp(jnp.maximum(-diff, 0.0) * lgh[4 + h:5 + h, 0:1])
        decay.append(jnp.where(diff > 0, df, jnp.where(diff < 0, db, 2.0)))
    rg = rg_ref[...]
    k_scale = HEAD_DIM ** -0.5

    def load(ref, r0):
        rows = pl.ds(r0, c)
        return ref[rows, 256:512], ref[rows, 512:768] * k_scale, ref[rows, 768:1024], ref[rows, 1024:1280]

    def increment(k, v, wk):
        return jnp.where(same_head, _dot_tn(k * wk, v), 0.0)

    def chunk_out(q, k, v, gate, s_fwd, s_bwd):
        o = _dot(q * wqf, s_fwd) + _dot(q * wqb, s_bwd)
        for h in range(4):
            sel = lane == h
            s = _dot_nt(jnp.where(sel, q, 0.0), k)
            o = o + _dot(s * decay[h], jnp.where(sel, v, 0.0))
        on = o * lax.rsqrt(_group_sums(o * o, ones) * (1.0 / HEAD_DIM) + EPS) * rg
        return (on * _silu(gate)).astype(BF16)

    _, kc, vc, _ = load(uc_ref, 0)
    s_fwd = increment(kc, vc, wkf)
    s_bwd = increment(kc, vc, wkb)

    for j in range(n_chunks - 1, -1, -1):
        tst_ref[j] = s_bwd
        if j > 0:
            _, k, v, _ = load(ul_ref, j * c)
            s_bwd = s_bwd * dcb + increment(k, v, wkb)

    for j in range(n_chunks):
        q, k, v, gate = load(ul_ref, j * c)
        ydl_ref[pl.ds(j * c, c), :] = chunk_out(q, k, v, gate, s_fwd, tst_ref[j])
        if j < n_chunks - 1:
            s_fwd = s_fwd * dcf + increment(k, v, wkf)

    if with_ctx:
        qc, kc, vc, gc = load(uc_ref, 0)
        zeros = jnp.zeros((256, 256), F32)
        ydc_ref[...] = chunk_out(qc, kc, vc, gc, zeros, zeros)


def _mixer(ucd, decay_lanes, decay_heads, ret_gain_lanes, pool_w_bd, pool_scale, *, with_ctx):
    ctx_blk0 = T_LAT // CTX_LEN
    const = lambda b: (0, 0)
    out_specs = [pl.BlockSpec((SEQ, 256), lambda b: (b, 0))] * 2
    out_shape = [jax.ShapeDtypeStruct((T_LAT, 256), BF16)] * 2
    if with_ctx:
        out_specs = out_specs + [pl.BlockSpec((CTX_LEN, 256), lambda b: (b, 0))] * 2
        out_shape = out_shape + [jax.ShapeDtypeStruct((T_CTX, 256), BF16)] * 2
    return pl.pallas_call(
        functools.partial(_mix_kernel, with_ctx=with_ctx),
        grid=(BATCH,),
        in_specs=[
            pl.BlockSpec((SEQ, 1280), lambda b: (b, 0)),
            pl.BlockSpec((CTX_LEN, 1280), lambda b: (ctx_blk0 + b, 0)),
            pl.BlockSpec((2, 256), const),
            pl.BlockSpec((8, 128), const),
            pl.BlockSpec((1, 256), const),
            pl.BlockSpec((256, 256), const),
            pl.BlockSpec((1, 256), const),
        ],
        out_specs=out_specs,
        out_shape=out_shape,
        scratch_shapes=[pltpu.VMEM((SEQ // RET_C, 256, 256), F32)]
        + [pltpu.VMEM((SEQ + 2 * POOL_PAD, 128), F32)] * 3,
        compiler_params=_cparams(("parallel",)),
        name="pool_retention_mixer",
    )(ucd, ucd, decay_lanes, decay_heads, ret_gain_lanes, pool_w_bd, pool_scale)


def _rope_tables(dim, tm):
    q = dim // 4
    n_rows = SEQ // GRID_W
    rows = np.repeat(np.arange(n_rows), GRID_W).astype(np.float64)
    cols = np.tile(np.arange(GRID_W), n_rows).astype(np.float64)
    inv = ROPE_BASE ** (-np.arange(q, dtype=np.float64) / q)
    ar, ac = rows[:, None] * inv, cols[:, None] * inv
    cos = np.concatenate([np.cos(ar)] * 2 + [np.cos(ac)] * 2, axis=-1)
    sin = np.concatenate([np.sin(ar)] * 2 + [np.sin(ac)] * 2, axis=-1)
    first_half = np.tile(np.repeat(np.array([True, False]), q), 2)
    tabs = np.stack([cos, np.where(first_half, -sin, 0.0), np.where(first_half, 0.0, sin)])
    tabs = np.tile(tabs, (1, 1, 128 // dim))
    ident = np.stack([np.ones((tm, 128)), np.zeros((tm, 128)), np.zeros((tm, 128))])
    return jnp.asarray(np.concatenate([tabs, ident], axis=1), dtype=F32)


def _block_diag(blocks):
    n, d, _ = blocks.shape
    eye = jnp.eye(n, dtype=blocks.dtype)
    return jnp.einsum('gcd,gh->gchd', blocks, eye).reshape(n * d, n * d)


def kernel(x, c, ctx, c_ctx, w_mod, b_mod, norm_gain, ffn_w_gate, ffn_w_up, ffn_w_down, w_in, w_out,
           attn_qk_gain, diff_lambda, diff_out_gain, pool_w, pool_scale, ret_decay_logit, ret_out_gain):
    xs = (x.reshape(T_LAT, D_MODEL), ctx.reshape(T_CTX, D_MODEL))
    cc = jnp.concatenate([c, c_ctx[None, :], jnp.zeros((MOD_ROWS - BATCH - 1, D_MODEL), F32)], axis=0)
    mod_all = _modulation(cc, w_mod, b_mod).reshape(DEPTH, MOD_ROWS, 9, D_MODEL)
    tab_a = _rope_tables(HEAD_DIM, TM_PROJ)
    tab_b = _rope_tables(DIFF_QK_DIM, TM_PROJ)

    for i in range(DEPTH):
        last = i == DEPTH - 1
        mod = mod_all[i]
        gains = norm_gain[i]
        qk_gain_lanes = jnp.tile(attn_qk_gain[i], (1, 256 // HEAD_DIM))
        diff_gain_lanes = jnp.tile(diff_out_gain[i], 256 // HEAD_DIM)[None, :]
        ret_gain_lanes = jnp.tile(ret_out_gain[i], 256 // HEAD_DIM)[None, :]
        decay_lanes = jnp.repeat(ret_decay_logit[i], HEAD_DIM, axis=1)
        decay_heads = jnp.broadcast_to(ret_decay_logit[i].reshape(8, 1), (8, 128))
        pool_w_bd = _block_diag(pool_w[i])
        lam_init = 0.8 - 0.6 * math.exp(-0.3 * i)

        def ffn(hs, sub, rows, **mix):
            return _ffn_sublayer(hs, mod, gains, ffn_w_gate, ffn_w_up, ffn_w_down, layer=i, sub=sub, rows=rows,
                                 **mix)

        h = ffn(xs, 0, T_ALL)
        qa, ka, va, qb, kb, vb, ucd = _in_projection(h, mod, gains, w_in, tab_a, tab_b, qk_gain_lanes, layer=i)
        ya, yb = _attention(qa, ka, va, qb, kb, vb, diff_lambda[i], diff_gain_lanes, lam_init=lam_init, latent=True)
        mixed = _mixer(ucd, decay_lanes, decay_heads, ret_gain_lanes, pool_w_bd, pool_scale[i][None, :],
                       with_ctx=not last)
        if last:
            ys, rows = [(ya,), (yb,), (mixed[0],), (mixed[1],)], T_LAT
        else:
            ya_c, yb_c = _attention(qa, ka, va, qb, kb, vb, diff_lambda[i], diff_gain_lanes, lam_init=lam_init,
                                    latent=False)
            ys, rows = [(ya, ya_c), (yb, yb_c), (mixed[0], mixed[2]), (mixed[1], mixed[3])], T_ALL
        h = ffn((h,), 2, rows, ys=ys, w_out=w_out)
        xs = (h,)
    return h.reshape(BATCH, SEQ, D_MODEL)
```

```python
import functools
import math

import jax
import jax.numpy as jnp
import numpy as np
from jax import lax
from jax.experimental import pallas as pl
from jax.experimental.pallas import tpu as pltpu

F32 = jnp.float32
BF16 = jnp.bfloat16

D_MODEL = 1024
BATCH = 8
SEQ = 2048
DEPTH = 2
CTX_LEN = 256
GRID_W = 64
HEAD_DIM = 64
DIFF_QK_DIM = 32
GROUP_WIDTH = 256
D_FF = 2816
D_IN = 2560
FFN_RESIDUAL = 0.5
ROPE_BASE = 10000.0
EPS = 1e-6

T_LAT = BATCH * SEQ
T_CTX = BATCH * CTX_LEN
T_ALL = T_LAT + T_CTX
MOD_ROWS = 16

TM_FFN = 512
FFN_SPLIT = 2
TM_PROJ = 1024
LOG2E = 1.4426950408889634
TQ = 256
RET_C = 256
VMEM_LIMIT = 56 * 1024 * 1024


def _cparams(sem):
    return pltpu.CompilerParams(dimension_semantics=sem, vmem_limit_bytes=VMEM_LIMIT)


def _silu(x):
    return x * jax.nn.sigmoid(x)


_MXU = dict(preferred_element_type=F32, precision=lax.Precision.DEFAULT)


def _dot(a, b):
    return jnp.dot(a, b, **_MXU)


def _dot_nt(a, b):
    return lax.dot_general(a, b, (((1,), (1,)), ((), ())), **_MXU)


def _dot_tn(a, b):
    return lax.dot_general(a, b, (((0,), (0,)), ((), ())), **_MXU)


def _block_ones(n, blk):
    r = lax.broadcasted_iota(jnp.int32, (n, n), 0) // blk
    c = lax.broadcasted_iota(jnp.int32, (n, n), 1) // blk
    return r == c


def _group_sums(sq, ones):
    return _dot(sq, ones)


def _rms_rows(x):
    return lax.rsqrt(jnp.mean(x * x, axis=-1, keepdims=True) + EPS)


def _mod_row_index(tile, tm):
    return jnp.minimum((tile * tm) // SEQ, BATCH)


def _mod_kernel(cc_ref, w_ref, b_ref, o_ref):
    o_ref[...] = _dot(_silu(cc_ref[...]), w_ref[...]) + b_ref[...]


def _modulation(cc, w_mod, b_mod):
    tn = 2304
    n_out = 9 * D_MODEL
    return pl.pallas_call(
        _mod_kernel,
        grid=(DEPTH, n_out // tn),
        in_specs=[
            pl.BlockSpec((MOD_ROWS, D_MODEL), lambda l, j: (0, 0)),
            pl.BlockSpec((None, D_MODEL, tn), lambda l, j: (l, 0, j)),
            pl.BlockSpec((None, 1, tn), lambda l, j: (l, 0, j)),
        ],
        out_specs=pl.BlockSpec((None, MOD_ROWS, tn), lambda l, j: (l, 0, j)),
        out_shape=jax.ShapeDtypeStruct((DEPTH, MOD_ROWS, n_out), F32),
        compiler_params=_cparams(("parallel", "parallel")),
        name="adaln_modulation",
    )(cc, w_mod, b_mod.reshape(DEPTH, 1, n_out))


def _token_tile(refs, rows, lat_tiles):
    if len(refs) == 1:
        return refs[0][rows, :]
    return jnp.where(pl.program_id(0) < lat_tiles, refs[0][rows, :], refs[1][rows, :])


def _ffn_kernel(*refs, n_x, n_y, sub, g_pre, g_post, lat_tiles):
    x_refs = refs[:n_x]
    m_ref, g_ref = refs[n_x:n_x + 2]
    y_refs = refs[n_x + 2:n_x + 2 + 4 * n_y]
    wo_ref = refs[n_x + 2 + 4 * n_y] if n_y else None
    wg_ref, wu_ref, wd_ref, o_ref = refs[-4:]
    pre = g_ref[g_pre:g_pre + 1, :] * (1.0 + m_ref[3 * sub + 1:3 * sub + 2, :])
    shift = m_ref[3 * sub:3 * sub + 1, :]
    post = (FFN_RESIDUAL * m_ref[3 * sub + 2:3 * sub + 3, :]) * g_ref[g_post:g_post + 1, :]
    mix_post = m_ref[5:6, :] * g_ref[3:4, :]
    rs = o_ref.shape[0] // FFN_SPLIT
    for s in range(FFN_SPLIT):
        rows = pl.ds(s * rs, rs)
        x = _token_tile(x_refs, rows, lat_tiles)
        if n_y:
            mixed = jnp.concatenate([_token_tile(y_refs[n_y * j:n_y * (j + 1)], rows, lat_tiles) for j in range(4)],
                                    axis=1)
            y = _dot(mixed.astype(F32), wo_ref[...])
            x = x + (y * _rms_rows(y)) * mix_post
        hm = x * _rms_rows(x) * pre + shift
        gate = _dot(hm, wg_ref[...])
        up = _dot(hm, wu_ref[...])
        y = _dot(_silu(gate) * up, wd_ref[...])
        o_ref[rows, :] = x + (y * _rms_rows(y)) * post


def _stream_specs(xs, tm, width):
    if len(xs) == 1:
        return [pl.BlockSpec((tm, width), lambda i: (i, 0))]
    lat_tiles = xs[0].shape[0] // tm
    return [pl.BlockSpec((tm, width), lambda i: (jnp.minimum(i, lat_tiles - 1), 0)),
            pl.BlockSpec((tm, width), lambda i: (jnp.maximum(i - lat_tiles, 0), 0))]


def _ffn_sublayer(xs, mod, gains, wg, wu, wd, *, layer, sub, rows, ys=(), w_out=None):
    tm = TM_FFN
    n_y = len(ys[0]) if ys else 0
    kern = functools.partial(_ffn_kernel, n_x=len(xs), n_y=n_y, sub=sub, g_pre=2 * sub, g_post=2 * sub + 1,
                             lat_tiles=T_LAT // tm)
    k = sub // 2
    resident = dict(pipeline_mode=pl.Buffered(1))
    y_specs, y_args = [], []
    for parts in ys:
        y_specs += _stream_specs(parts, tm, 256)
        y_args += list(parts)
    if ys:
        y_specs.append(pl.BlockSpec((None, D_MODEL, D_MODEL), lambda i: (layer, 0, 0), **resident))
        y_args.append(w_out)
    return pl.pallas_call(
        kern,
        grid=(rows // tm,),
        in_specs=_stream_specs(xs, tm, D_MODEL) + [
            pl.BlockSpec((None, 9, D_MODEL), lambda i: (_mod_row_index(i, tm), 0, 0)),
            pl.BlockSpec((6, D_MODEL), lambda i: (0, 0)),
        ] + y_specs + [
            pl.BlockSpec((None, None, D_MODEL, D_FF), lambda i: (layer, k, 0, 0), **resident),
            pl.BlockSpec((None, None, D_MODEL, D_FF), lambda i: (layer, k, 0, 0), **resident),
            pl.BlockSpec((None, None, D_FF, D_MODEL), lambda i: (layer, k, 0, 0), **resident),
        ],
        out_specs=pl.BlockSpec((tm, D_MODEL), lambda i: (i, 0)),
        out_shape=jax.ShapeDtypeStruct((rows, D_MODEL), F32),
        compiler_params=_cparams(("parallel",)),
        name=f"ffn_sublayer_{sub}",
    )(*xs, mod, gains, *y_args, wg, wu, wd)


def _rope(x, tab_ref, rows, quarter):
    w = x.shape[1]
    reps = w // tab_ref.shape[2]

    def tab(k):
        t = tab_ref[k, rows, :]
        return t if reps == 1 else jnp.concatenate([t] * reps, axis=1)

    return x * tab(0) + pltpu.roll(x, w - quarter, 1) * tab(1) + pltpu.roll(x, quarter, 1) * tab(2)


def _swap_middle_heads(x):
    first = lax.broadcasted_iota(jnp.int32, (1, 128), 1) < HEAD_DIM
    lo, hi = x[:, 0:128], x[:, 128:256]
    return jnp.concatenate([jnp.where(first, lo, pltpu.roll(hi, HEAD_DIM, 1)),
                            jnp.where(first, pltpu.roll(lo, HEAD_DIM, 1), hi)], axis=1)


def _inproj_kernel(x_ref, m_ref, g_ref, w_ref, ta_ref, tb_ref, qkg_ref,
                   qa_ref, ka_ref, va_ref, qb_ref, kb_ref, vb_ref, ucd_ref):
    pre = g_ref[2:3, :] * (1.0 + m_ref[4:5, :])
    shift = m_ref[3:4, :]
    ones = _block_ones(256, HEAD_DIM).astype(F32)
    inv_d = 1.0 / HEAD_DIM
    lane128 = lax.broadcasted_iota(jnp.int32, (1, 128), 1)
    rows = pl.ds(0, x_ref.shape[0])
    x = x_ref[...]
    hm = x * _rms_rows(x) * pre + shift
    u_ab = _dot(hm, w_ref[:, 0:1280])

    def proj(lo, hi):
        return u_ab[:, lo:hi]

    q = proj(0, 256)
    q = q * lax.rsqrt(_group_sums(q * q, ones) * inv_d + EPS) * qkg_ref[0:1, :]
    q = _rope(q, ta_ref, rows, HEAD_DIM // 4) * (HEAD_DIM ** -0.5 * LOG2E)
    qa_ref[...] = _swap_middle_heads(q).astype(BF16)
    k = proj(256, 384)
    k = k * lax.rsqrt(_group_sums(k * k, ones[:128, :128]) * inv_d + EPS) * qkg_ref[1:2, :128]
    ka_ref[...] = _rope(k, ta_ref, rows, HEAD_DIM // 4).astype(BF16)
    v = proj(384, 512)
    for g in range(2):
        va_ref[:, 128 * g:128 * g + 128] = jnp.where(lane128 == HEAD_DIM * (1 - g), 1.0, v).astype(BF16)

    qb_ref[...] = (_rope(proj(512, 768), tb_ref, rows, DIFF_QK_DIM // 4)
                   * (DIFF_QK_DIM ** -0.5 * LOG2E)).astype(BF16)
    kb_ref[...] = _rope(proj(768, 1024), tb_ref, rows, DIFF_QK_DIM // 4).astype(BF16)
    v = proj(1024, 1280)
    for hd in range(4):
        own = (lane128 // HEAD_DIM) == hd % 2
        ones_col = jnp.where(lane128 == HEAD_DIM * (1 - hd % 2), 1.0, 0.0)
        pair = v[:, 128 * (hd // 2):128 * (hd // 2) + 128]
        vb_ref[:, 128 * hd:128 * hd + 128] = jnp.where(own, pair, ones_col).astype(BF16)

    ucd_ref[...] = _dot(hm, w_ref[:, 1280:2560])


def _in_projection(h, mod, gains, w_in, tab_a, tab_b, qk_gain_lanes, *, layer):
    tm = TM_PROJ
    lat_tiles = T_LAT // tm
    tiles_per_seq = SEQ // tm

    def tab_idx(i):
        return jnp.where(i < lat_tiles, i % tiles_per_seq, tiles_per_seq)

    def row_spec(width):
        return pl.BlockSpec((tm, width), lambda i: (i, 0))

    def out(width, dtype):
        return jax.ShapeDtypeStruct((T_ALL, width), dtype)

    return pl.pallas_call(
        _inproj_kernel,
        grid=(T_ALL // tm,),
        in_specs=[
            row_spec(D_MODEL),
            pl.BlockSpec((None, 9, D_MODEL), lambda i: (_mod_row_index(i, tm), 0, 0)),
            pl.BlockSpec((6, D_MODEL), lambda i: (0, 0)),
            pl.BlockSpec((None, D_MODEL, D_IN), lambda i: (layer, 0, 0), pipeline_mode=pl.Buffered(1)),
            pl.BlockSpec((3, tm, 128), lambda i: (0, tab_idx(i), 0)),
            pl.BlockSpec((3, tm, 128), lambda i: (0, tab_idx(i), 0)),
            pl.BlockSpec((2, 256), lambda i: (0, 0)),
        ],
        out_specs=[row_spec(256), row_spec(128), row_spec(256), row_spec(256), row_spec(256), row_spec(512),
                   row_spec(1280)],
        out_shape=[out(256, BF16), out(128, BF16), out(256, BF16), out(256, BF16), out(256, BF16), out(512, BF16),
                   out(1280, F32)],
        compiler_params=_cparams(("parallel",)),
        name="in_projection",
    )(h, mod, gains, w_in, tab_a, tab_b, qk_gain_lanes)


def _softmax_pv(q_blocks, k_refs, v_refs, lanes_of, ones_lane_of):
    tq = q_blocks[0].shape[0]
    qs = jnp.concatenate(q_blocks, axis=0)
    ss = [_dot_nt(qs, k[...]) for k in k_refs]
    m = ss[0].max(axis=-1, keepdims=True)
    for s in ss[1:]:
        m = jnp.maximum(m, s.max(axis=-1, keepdims=True))
    ps = [jnp.exp2(s - m).astype(BF16) for s in ss]
    outs = []
    for b in range(len(q_blocks)):
        o = None
        for p, v in zip(ps, v_refs):
            pv = _dot(p[tq * b:tq * (b + 1)], v[:, lanes_of(b)])
            o = pv if o is None else o + pv
        lane = ones_lane_of(b)
        outs.append(o / o[:, lane:lane + 1])
    return outs


def _attn_kernel(*refs, n_parts, lam_init):
    qa_ref = refs[0]
    ka = refs[1:1 + n_parts]
    va = refs[1 + n_parts:1 + 2 * n_parts]
    qb_ref = refs[1 + 2 * n_parts]
    kb = refs[2 + 2 * n_parts:2 + 3 * n_parts]
    vb = refs[2 + 3 * n_parts:2 + 4 * n_parts]
    lam_ref, dg_ref, ya_ref, yb_ref = refs[2 + 4 * n_parts:]

    lane128 = lax.broadcasted_iota(jnp.int32, (1, 128), 1)
    first = lane128 < HEAD_DIM
    qa = qa_ref[...]
    zero = jnp.zeros((), BF16)
    blocks = [jnp.where((lane128 // HEAD_DIM) == g, qa[:, 128 * j:128 * j + 128], zero)
              for g in range(2) for j in range(2)]
    outs = _softmax_pv(blocks, ka, va, lambda b: slice(128 * (b // 2), 128 * (b // 2) + 128),
                       lambda b: HEAD_DIM * (1 - b // 2))
    y0 = jnp.where(first, outs[0], outs[2])
    y1 = jnp.where(first, outs[1], outs[3])
    ya_ref[...] = _swap_middle_heads(jnp.concatenate([y0, y1], axis=1)).astype(BF16)

    lp = lam_ref[...]
    lam = (jnp.exp(jnp.sum(lp[0:1] * lp[1:2], axis=-1, keepdims=True))
           - jnp.exp(jnp.sum(lp[2:3] * lp[3:4], axis=-1, keepdims=True)) + lam_init)
    half = lax.broadcasted_iota(jnp.int32, (1, 256), 1) // DIFF_QK_DIM
    qb = qb_ref[...]
    blocks = [jnp.where(half == b, qb, zero) for b in range(8)]
    outs = _softmax_pv(blocks, kb, vb, lambda b: slice(128 * (b // 2), 128 * (b // 2) + 128),
                       lambda b: HEAD_DIM * (1 - (b // 2) % 2))
    d = [outs[2 * h] - lam * outs[2 * h + 1] for h in range(4)]
    y = jnp.concatenate([jnp.where(first, d[0], d[1]), jnp.where(first, d[2], d[3])], axis=1)
    ones = _block_ones(256, HEAD_DIM).astype(F32)
    yn = y * lax.rsqrt(_group_sums(y * y, ones) * (1.0 / HEAD_DIM) + EPS) * dg_ref[...]
    yb_ref[...] = (yn * (1.0 - lam_init)).astype(BF16)


def _attention(qa, ka, va, qb, kb, vb, diff_lambda, diff_gain_lanes, *, lam_init, latent):
    ctx_blk0 = T_LAT // CTX_LEN
    if latent:
        grid = (BATCH, SEQ // TQ)
        q_map = lambda b, i: (b * (SEQ // TQ) + i, 0)
        kv_specs = lambda w: [pl.BlockSpec((CTX_LEN, w), lambda b, i: (ctx_blk0 + b, 0)),
                              pl.BlockSpec((SEQ, w), lambda b, i: (b, 0))]
        rows = T_LAT
        n_parts = 2
        const = lambda b, i: (0, 0)
        o_map = q_map
    else:
        grid = (BATCH,)
        q_map = lambda b: (ctx_blk0 + b, 0)
        kv_specs = lambda w: [pl.BlockSpec((CTX_LEN, w), lambda b: (ctx_blk0 + b, 0))]
        rows = T_CTX
        n_parts = 1
        const = lambda b: (0, 0)
        o_map = lambda b: (b, 0)
    in_specs = ([pl.BlockSpec((TQ, 256), q_map)] + kv_specs(128) + kv_specs(256)
                + [pl.BlockSpec((TQ, 256), q_map)] + kv_specs(256) + kv_specs(512)
                + [pl.BlockSpec((4, DIFF_QK_DIM), const), pl.BlockSpec((1, 256), const)])
    args = [qa] + [ka] * n_parts + [va] * n_parts + [qb] + [kb] * n_parts + [vb] * n_parts
    return pl.pallas_call(
        functools.partial(_attn_kernel, n_parts=n_parts, lam_init=lam_init),
        grid=grid,
        in_specs=in_specs,
        out_specs=[pl.BlockSpec((TQ, 256), o_map), pl.BlockSpec((TQ, 256), o_map)],
        out_shape=[jax.ShapeDtypeStruct((rows, 256), BF16), jax.ShapeDtypeStruct((rows, 256), BF16)],
        compiler_params=_cparams(("parallel",) * len(grid)),
        name="attention_latent" if latent else "attention_context",
    )(*args, diff_lambda, diff_gain_lanes)


POOL_PAD = 16


def _pool(u_ref, n_seq, pw, ps, pa, pb, pc):
    pad = POOL_PAD
    n = n_seq + pad
    z = jnp.zeros((pad, 128), F32)
    first = lax.broadcasted_iota(jnp.int32, (1, 128), 1) < 64

    def shifted_sum(src, dst, k):
        dst[0:n, :] = src[0:n, :] + src[k:n + k, :]

    tiles = []
    for tile in range(2):
        pa[0:pad, :] = z
        pa[pad:n, :] = u_ref[0:n_seq, 128 * tile:128 * tile + 128]
        pa[n:n + pad, :] = z
        pb[n:n + pad, :] = z
        pc[n:n + pad, :] = z
        shifted_sum(pa, pb, 1)
        shifted_sum(pb, pc, 2)
        if tile == 0:
            w_lo = pb[pad - 1:pad - 1 + n_seq, :]
            w_hi = pc[pad - 2:pad - 2 + n_seq, :]
        else:
            shifted_sum(pc, pb, 4)
            w_lo = pb[pad - 4:pad - 4 + n_seq, :]
            shifted_sum(pb, pc, 8)
            w_hi = pc[pad - 8:pad - 8 + n_seq, :]
        tiles.append(jnp.where(first, w_lo, w_hi))
    wsum = jnp.concatenate(tiles, axis=1)
    grp = lax.broadcasted_iota(jnp.int32, (1, 256), 1) // 64
    back = jnp.where(grp == 0, 1, jnp.where(grp == 1, 2, jnp.where(grp == 2, 4, 8)))
    t = lax.broadcasted_iota(jnp.int32, (n_seq, 256), 0)
    count = jnp.minimum(t + (back - 1), n_seq - 1) - jnp.maximum(t - back, 0) + 1
    pooled = wsum / count.astype(F32) - u_ref[0:n_seq, 0:256]
    return _dot(pooled, pw) * ps


def _log_sigmoid(x):
    return jnp.minimum(x, 0.0) - jnp.log(1.0 + jnp.exp(-jnp.abs(x)))


def _mix_kernel(*refs, with_ctx):
    if with_ctx:
        (ul_ref, uc_ref, dl_ref, dh_ref, rg_ref, pw_ref, ps_ref,
         ycl_ref, ydl_ref, ycc_ref, ydc_ref, tst_ref, pa, pb, pc) = refs
    else:
        (ul_ref, uc_ref, dl_ref, dh_ref, rg_ref, pw_ref, ps_ref, ycl_ref, ydl_ref, tst_ref, pa, pb, pc) = refs
    c = RET_C
    n_chunks = SEQ // c
    pw = pw_ref[...]
    ps = ps_ref[...]

    ycl_ref[...] = _pool(ul_ref, SEQ, pw, ps, pa, pb, pc).astype(BF16)
    if with_ctx:
        ycc_ref[...] = _pool(uc_ref, CTX_LEN, pw, ps, pa, pb, pc).astype(BF16)

    lg = _log_sigmoid(dl_ref[...])
    lgh = _log_sigmoid(dh_ref[...])
    lgf, lgb = lg[0:1, :], lg[1:2, :]
    pos = lax.broadcasted_iota(jnp.int32, (c, 1), 0).astype(F32)
    wkf = jnp.exp((c - 1.0 - pos) * lgf)
    wkb = jnp.exp(pos * lgb)
    wqf = jnp.exp((pos + 1.0) * lgf)
    wqb = jnp.exp((c - pos) * lgb)
    dcf = jnp.exp(c * lgf)
    dcb = jnp.exp(c * lgb)
    same_head = _block_ones(256, HEAD_DIM)
    ones = same_head.astype(F32)
    lane = lax.broadcasted_iota(jnp.int32, (1, 256), 1) // HEAD_DIM
    diff = (lax.broadcasted_iota(jnp.int32, (c, c), 0) - lax.broadcasted_iota(jnp.int32, (c, c), 1)).astype(F32)
    decay = []
    for h in range(4):
        df = jnp.exp(jnp.maximum(diff, 0.0) * lgh[h:h + 1, 0:1])
        db = jnp.exp(jnp.maximum(-diff, 0.0) * lgh[4 + h:5 + h, 0:1])
        decay.append(jnp.where(diff > 0, df, jnp.where(diff < 0, db, 2.0)))
    rg = rg_ref[...]
    k_scale = HEAD_DIM ** -0.5

    def load(ref, r0):
        rows = pl.ds(r0, c)
        return ref[rows, 256:512], ref[rows, 512:768] * k_scale, ref[rows, 768:1024], ref[rows, 1024:1280]

    def increment(k, v, wk):
        return jnp.where(same_head, _dot_tn(k * wk, v), 0.0)

    def chunk_out(q, k, v, gate, s_fwd, s_bwd):
        o = _dot(q * wqf, s_fwd) + _dot(q * wqb, s_bwd)
        s = _dot_nt(jnp.concatenate([jnp.where(lane == h, q, 0.0) for h in range(4)], axis=0), k)
        p = jnp.concatenate([s[c * h:c * (h + 1)] * decay[h] for h in range(4)], axis=1)
        o = o + _dot(p, jnp.concatenate([jnp.where(lane == h, v, 0.0) for h in range(4)], axis=0))
        on = o * lax.rsqrt(_group_sums(o * o, ones) * (1.0 / HEAD_DIM) + EPS) * rg
        return (on * _silu(gate)).astype(BF16)

    _, kc, vc, _ = load(uc_ref, 0)
    s_fwd = increment(kc, vc, wkf)
    s_bwd = increment(kc, vc, wkb)

    for j in range(n_chunks - 1, -1, -1):
        tst_ref[j] = s_bwd
        if j > 0:
            _, k, v, _ = load(ul_ref, j * c)
            s_bwd = s_bwd * dcb + increment(k, v, wkb)

    for j in range(n_chunks):
        q, k, v, gate = load(ul_ref, j * c)
        ydl_ref[pl.ds(j * c, c), :] = chunk_out(q, k, v, gate, s_fwd, tst_ref[j])
        if j < n_chunks - 1:
            s_fwd = s_fwd * dcf + increment(k, v, wkf)

    if with_ctx:
        qc, kc, vc, gc = load(uc_ref, 0)
        zeros = jnp.zeros((256, 256), F32)
        ydc_ref[...] = chunk_out(qc, kc, vc, gc, zeros, zeros)


def _mixer(ucd, decay_lanes, decay_heads, ret_gain_lanes, pool_w_bd, pool_scale, *, with_ctx):
    ctx_blk0 = T_LAT // CTX_LEN
    const = lambda b: (0, 0)
    out_specs = [pl.BlockSpec((SEQ, 256), lambda b: (b, 0))] * 2
    out_shape = [jax.ShapeDtypeStruct((T_LAT, 256), BF16)] * 2
    if with_ctx:
        out_specs = out_specs + [pl.BlockSpec((CTX_LEN, 256), lambda b: (b, 0))] * 2
        out_shape = out_shape + [jax.ShapeDtypeStruct((T_CTX, 256), BF16)] * 2
    return pl.pallas_call(
        functools.partial(_mix_kernel, with_ctx=with_ctx),
        grid=(BATCH,),
        in_specs=[
            pl.BlockSpec((SEQ, 1280), lambda b: (b, 0)),
            pl.BlockSpec((CTX_LEN, 1280), lambda b: (ctx_blk0 + b, 0)),
            pl.BlockSpec((2, 256), const),
            pl.BlockSpec((8, 128), const),
            pl.BlockSpec((1, 256), const),
            pl.BlockSpec((256, 256), const),
            pl.BlockSpec((1, 256), const),
        ],
        out_specs=out_specs,
        out_shape=out_shape,
        scratch_shapes=[pltpu.VMEM((SEQ // RET_C, 256, 256), F32)]
        + [pltpu.VMEM((SEQ + 2 * POOL_PAD, 128), F32)] * 3,
        compiler_params=_cparams(("parallel",)),
        name="pool_retention_mixer",
    )(ucd, ucd, decay_lanes, decay_heads, ret_gain_lanes, pool_w_bd, pool_scale)


def _rope_tables(dim, tm):
    q = dim // 4
    n_rows = SEQ // GRID_W
    rows = np.repeat(np.arange(n_rows), GRID_W).astype(np.float64)
    cols = np.tile(np.arange(GRID_W), n_rows).astype(np.float64)
    inv = ROPE_BASE ** (-np.arange(q, dtype=np.float64) / q)
    ar, ac = rows[:, None] * inv, cols[:, None] * inv
    cos = np.concatenate([np.cos(ar)] * 2 + [np.cos(ac)] * 2, axis=-1)
    sin = np.concatenate([np.sin(ar)] * 2 + [np.sin(ac)] * 2, axis=-1)
    first_half = np.tile(np.repeat(np.array([True, False]), q), 2)
    tabs = np.stack([cos, np.where(first_half, -sin, 0.0), np.where(first_half, 0.0, sin)])
    tabs = np.tile(tabs, (1, 1, 128 // dim))
    ident = np.stack([np.ones((tm, 128)), np.zeros((tm, 128)), np.zeros((tm, 128))])
    return jnp.asarray(np.concatenate([tabs, ident], axis=1), dtype=F32)


def _block_diag(blocks):
    n, d, _ = blocks.shape
    eye = jnp.eye(n, dtype=blocks.dtype)
    return jnp.einsum('gcd,gh->gchd', blocks, eye).reshape(n * d, n * d)


def kernel(x, c, ctx, c_ctx, w_mod, b_mod, norm_gain, ffn_w_gate, ffn_w_up, ffn_w_down, w_in, w_out,
           attn_qk_gain, diff_lambda, diff_out_gain, pool_w, pool_scale, ret_decay_logit, ret_out_gain):
    xs = (x.reshape(T_LAT, D_MODEL), ctx.reshape(T_CTX, D_MODEL))
    cc = jnp.concatenate([c, c_ctx[None, :], jnp.zeros((MOD_ROWS - BATCH - 1, D_MODEL), F32)], axis=0)
    mod_all = _modulation(cc, w_mod, b_mod).reshape(DEPTH, MOD_ROWS, 9, D_MODEL)
    tab_a = _rope_tables(HEAD_DIM, TM_PROJ)
    tab_b = _rope_tables(DIFF_QK_DIM, TM_PROJ)

    for i in range(DEPTH):
        last = i == DEPTH - 1
        mod = mod_all[i]
        gains = norm_gain[i]
        qk_gain_lanes = jnp.tile(attn_qk_gain[i], (1, 256 // HEAD_DIM))
        diff_gain_lanes = jnp.tile(diff_out_gain[i], 256 // HEAD_DIM)[None, :]
        ret_gain_lanes = jnp.tile(ret_out_gain[i], 256 // HEAD_DIM)[None, :]
        decay_lanes = jnp.repeat(ret_decay_logit[i], HEAD_DIM, axis=1)
        decay_heads = jnp.broadcast_to(ret_decay_logit[i].reshape(8, 1), (8, 128))
        pool_w_bd = _block_diag(pool_w[i])
        lam_init = 0.8 - 0.6 * math.exp(-0.3 * i)

        def ffn(hs, sub, rows, **mix):
            return _ffn_sublayer(hs, mod, gains, ffn_w_gate, ffn_w_up, ffn_w_down, layer=i, sub=sub, rows=rows,
                                 **mix)

        h = ffn(xs, 0, T_ALL)
        qa, ka, va, qb, kb, vb, ucd = _in_projection(h, mod, gains, w_in, tab_a, tab_b, qk_gain_lanes, layer=i)
        ya, yb = _attention(qa, ka, va, qb, kb, vb, diff_lambda[i], diff_gain_lanes, lam_init=lam_init, latent=True)
        mixed = _mixer(ucd, decay_lanes, decay_heads, ret_gain_lanes, pool_w_bd, pool_scale[i][None, :],
                       with_ctx=not last)
        if last:
            ys, rows = [(ya,), (yb,), (mixed[0],), (mixed[1],)], T_LAT
        else:
            ya_c, yb_c = _attention(qa, ka, va, qb, kb, vb, diff_lambda[i], diff_gain_lanes, lam_init=lam_init,
                                    latent=False)
            ys, rows = [(ya, ya_c), (yb, yb_c), (mixed[0], mixed[2]), (mixed[1], mixed[3])], T_ALL
        h = ffn((h,), 2, rows, ys=ys, w_out=w_out)
        xs = (h,)
    return h.reshape(BATCH, SEQ, D_MODEL)
```

```python
import functools
import math

import jax
import jax.numpy as jnp
import numpy as np
from jax import lax
from jax.experimental import pallas as pl
from jax.experimental.pallas import tpu as pltpu

F32 = jnp.float32
BF16 = jnp.bfloat16

D_MODEL = 1024
BATCH = 8
SEQ = 2048
DEPTH = 2
CTX_LEN = 256
GRID_W = 64
HEAD_DIM = 64
DIFF_QK_DIM = 32
D_FF = 2816
D_IN = 2560
FFN_RESIDUAL = 0.5
ROPE_BASE = 10000.0
EPS = 1e-6

T_LAT = BATCH * SEQ
T_CTX = BATCH * CTX_LEN
T_ALL = T_LAT + T_CTX
MOD_ROWS = 16

TM_FFN = 512
FFN_SPLIT = 2
TM_PROJ = 1024
LOG2E = 1.4426950408889634
TQ = 256
RET_C = 256
VMEM_LIMIT = 56 * 1024 * 1024


def _cparams(sem):
    return pltpu.CompilerParams(dimension_semantics=sem, vmem_limit_bytes=VMEM_LIMIT)


def _silu(x):
    return x * jax.nn.sigmoid(x)


_MXU = dict(preferred_element_type=F32, precision=lax.Precision.DEFAULT)


def _dot(a, b):
    return jnp.dot(a, b, **_MXU)


def _dot_nt(a, b):
    return lax.dot_general(a, b, (((1,), (1,)), ((), ())), **_MXU)


def _dot_tn(a, b):
    return lax.dot_general(a, b, (((0,), (0,)), ((), ())), **_MXU)


def _block_ones(n, blk):
    r = lax.broadcasted_iota(jnp.int32, (n, n), 0) // blk
    c = lax.broadcasted_iota(jnp.int32, (n, n), 1) // blk
    return r == c


def _group_sums(sq, ones):
    return _dot(sq, ones)


def _rms_rows(x):
    return lax.rsqrt(jnp.mean(x * x, axis=-1, keepdims=True) + EPS)


def _mod_row_index(tile, tm):
    return jnp.minimum((tile * tm) // SEQ, BATCH)


def _mod_kernel(cc_ref, w_ref, b_ref, o_ref):
    o_ref[...] = _dot(_silu(cc_ref[...]), w_ref[...]) + b_ref[...]


def _modulation(cc, w_mod, b_mod):
    tn = 2304
    n_out = 9 * D_MODEL
    return pl.pallas_call(
        _mod_kernel,
        grid=(DEPTH, n_out // tn),
        in_specs=[
            pl.BlockSpec((MOD_ROWS, D_MODEL), lambda l, j: (0, 0)),
            pl.BlockSpec((None, D_MODEL, tn), lambda l, j: (l, 0, j)),
            pl.BlockSpec((None, 1, tn), lambda l, j: (l, 0, j)),
        ],
        out_specs=pl.BlockSpec((None, MOD_ROWS, tn), lambda l, j: (l, 0, j)),
        out_shape=jax.ShapeDtypeStruct((DEPTH, MOD_ROWS, n_out), F32),
        compiler_params=_cparams(("parallel", "parallel")),
        name="adaln_modulation",
    )(cc, w_mod, b_mod.reshape(DEPTH, 1, n_out))


def _token_tile(refs, rows, lat_tiles):
    if len(refs) == 1:
        return refs[0][rows, :]
    return jnp.where(pl.program_id(0) < lat_tiles, refs[0][rows, :], refs[1][rows, :])


def _ffn_kernel(*refs, n_x, n_y, sub, g_pre, g_post, lat_tiles):
    x_refs = refs[:n_x]
    m_ref, g_ref = refs[n_x:n_x + 2]
    y_refs = refs[n_x + 2:n_x + 2 + 4 * n_y]
    wo_ref = refs[n_x + 2 + 4 * n_y] if n_y else None
    wg_ref, wu_ref, wd_ref, o_ref = refs[-4:]
    pre = g_ref[g_pre:g_pre + 1, :] * (1.0 + m_ref[3 * sub + 1:3 * sub + 2, :])
    shift = m_ref[3 * sub:3 * sub + 1, :]
    post = (FFN_RESIDUAL * m_ref[3 * sub + 2:3 * sub + 3, :]) * g_ref[g_post:g_post + 1, :]
    mix_post = m_ref[5:6, :] * g_ref[3:4, :]
    rs = o_ref.shape[0] // FFN_SPLIT
    for s in range(FFN_SPLIT):
        rows = pl.ds(s * rs, rs)
        x = _token_tile(x_refs, rows, lat_tiles)
        if n_y:
            mixed = jnp.concatenate([_token_tile(y_refs[n_y * j:n_y * (j + 1)], rows, lat_tiles) for j in range(4)],
                                    axis=1)
            y = _dot(mixed.astype(F32), wo_ref[...])
            x = x + (y * _rms_rows(y)) * mix_post
        hm = x * _rms_rows(x) * pre + shift
        gate = _dot(hm, wg_ref[...])
        up = _dot(hm, wu_ref[...])
        y = _dot(_silu(gate) * up, wd_ref[...])
        o_ref[rows, :] = x + (y * _rms_rows(y)) * post


def _stream_specs(xs, tm, width):
    if len(xs) == 1:
        return [pl.BlockSpec((tm, width), lambda i: (i, 0))]
    lat_tiles = xs[0].shape[0] // tm
    return [pl.BlockSpec((tm, width), lambda i: (jnp.minimum(i, lat_tiles - 1), 0)),
            pl.BlockSpec((tm, width), lambda i: (jnp.maximum(i - lat_tiles, 0), 0))]


def _ffn_sublayer(xs, mod, gains, wg, wu, wd, *, layer, sub, rows, ys=(), w_out=None):
    tm = TM_FFN
    n_y = len(ys[0]) if ys else 0
    kern = functools.partial(_ffn_kernel, n_x=len(xs), n_y=n_y, sub=sub, g_pre=2 * sub, g_post=2 * sub + 1,
                             lat_tiles=T_LAT // tm)
    k = sub // 2
    resident = dict(pipeline_mode=pl.Buffered(1))
    y_specs, y_args = [], []
    for parts in ys:
        y_specs += _stream_specs(parts, tm, 256)
        y_args += list(parts)
    if ys:
        y_specs.append(pl.BlockSpec((None, D_MODEL, D_MODEL), lambda i: (layer, 0, 0), **resident))
        y_args.append(w_out)
    return pl.pallas_call(
        kern,
        grid=(rows // tm,),
        in_specs=_stream_specs(xs, tm, D_MODEL) + [
            pl.BlockSpec((None, 9, D_MODEL), lambda i: (_mod_row_index(i, tm), 0, 0)),
            pl.BlockSpec((6, D_MODEL), lambda i: (0, 0)),
        ] + y_specs + [
            pl.BlockSpec((None, None, D_MODEL, D_FF), lambda i: (layer, k, 0, 0), **resident),
            pl.BlockSpec((None, None, D_MODEL, D_FF), lambda i: (layer, k, 0, 0), **resident),
            pl.BlockSpec((None, None, D_FF, D_MODEL), lambda i: (layer, k, 0, 0), **resident),
        ],
        out_specs=pl.BlockSpec((tm, D_MODEL), lambda i: (i, 0)),
        out_shape=jax.ShapeDtypeStruct((rows, D_MODEL), F32),
        compiler_params=_cparams(("parallel",)),
        name=f"ffn_sublayer_{sub}",
    )(*xs, mod, gains, *y_args, wg, wu, wd)


def _rope(x, tab_ref, rows, quarter):
    w = x.shape[1]
    reps = w // tab_ref.shape[2]

    def tab(k):
        t = tab_ref[k, rows, :]
        return t if reps == 1 else jnp.concatenate([t] * reps, axis=1)

    return x * tab(0) + pltpu.roll(x, w - quarter, 1) * tab(1) + pltpu.roll(x, quarter, 1) * tab(2)


def _swap_middle_heads(x):
    first = lax.broadcasted_iota(jnp.int32, (1, 128), 1) < HEAD_DIM
    lo, hi = x[:, 0:128], x[:, 128:256]
    return jnp.concatenate([jnp.where(first, lo, pltpu.roll(hi, HEAD_DIM, 1)),
                            jnp.where(first, pltpu.roll(lo, HEAD_DIM, 1), hi)], axis=1)


def _inproj_kernel(x_ref, m_ref, g_ref, w_ref, ta_ref, tb_ref, qkg_ref,
                   qa_ref, ka_ref, va_ref, qb_ref, kb_ref, vb_ref, ucd_ref):
    pre = g_ref[2:3, :] * (1.0 + m_ref[4:5, :])
    shift = m_ref[3:4, :]
    ones = _block_ones(256, HEAD_DIM).astype(F32)
    inv_d = 1.0 / HEAD_DIM
    lane128 = lax.broadcasted_iota(jnp.int32, (1, 128), 1)
    rows = pl.ds(0, x_ref.shape[0])
    x = x_ref[...]
    hm = x * _rms_rows(x) * pre + shift
    u_ab = _dot(hm, w_ref[:, 0:1280])

    def proj(lo, hi):
        return u_ab[:, lo:hi]

    q = proj(0, 256)
    q = q * lax.rsqrt(_group_sums(q * q, ones) * inv_d + EPS) * qkg_ref[0:1, :]
    q = _rope(q, ta_ref, rows, HEAD_DIM // 4) * (HEAD_DIM ** -0.5 * LOG2E)
    qa_ref[...] = _swap_middle_heads(q).astype(BF16)
    k = proj(256, 384)
    k = k * lax.rsqrt(_group_sums(k * k, ones[:128, :128]) * inv_d + EPS) * qkg_ref[1:2, :128]
    ka_ref[...] = _rope(k, ta_ref, rows, HEAD_DIM // 4).astype(BF16)
    v = proj(384, 512)
    for g in range(2):
        va_ref[:, 128 * g:128 * g + 128] = jnp.where(lane128 == HEAD_DIM * (1 - g), 1.0, v).astype(BF16)

    qb_ref[...] = (_rope(proj(512, 768), tb_ref, rows, DIFF_QK_DIM // 4)
                   * (DIFF_QK_DIM ** -0.5 * LOG2E)).astype(BF16)
    kb_ref[...] = _rope(proj(768, 1024), tb_ref, rows, DIFF_QK_DIM // 4).astype(BF16)
    v = proj(1024, 1280)
    for hd in range(4):
        own = (lane128 // HEAD_DIM) == hd % 2
        ones_col = jnp.where(lane128 == HEAD_DIM * (1 - hd % 2), 1.0, 0.0)
        pair = v[:, 128 * (hd // 2):128 * (hd // 2) + 128]
        vb_ref[:, 128 * hd:128 * hd + 128] = jnp.where(own, pair, ones_col).astype(BF16)

    ucd_ref[...] = _dot(hm, w_ref[:, 1280:2560])


def _in_projection(h, mod, gains, w_in, tab_a, tab_b, qk_gain_lanes, *, layer):
    tm = TM_PROJ
    lat_tiles = T_LAT // tm
    tiles_per_seq = SEQ // tm

    def tab_idx(i):
        return jnp.where(i < lat_tiles, i % tiles_per_seq, tiles_per_seq)

    def row_spec(width):
        return pl.BlockSpec((tm, width), lambda i: (i, 0))

    def out(width, dtype):
        return jax.ShapeDtypeStruct((T_ALL, width), dtype)

    return pl.pallas_call(
        _inproj_kernel,
        grid=(T_ALL // tm,),
        in_specs=[
            row_spec(D_MODEL),
            pl.BlockSpec((None, 9, D_MODEL), lambda i: (_mod_row_index(i, tm), 0, 0)),
            pl.BlockSpec((6, D_MODEL), lambda i: (0, 0)),
            pl.BlockSpec((None, D_MODEL, D_IN), lambda i: (layer, 0, 0), pipeline_mode=pl.Buffered(1)),
            pl.BlockSpec((3, tm, 128), lambda i: (0, tab_idx(i), 0)),
            pl.BlockSpec((3, tm, 128), lambda i: (0, tab_idx(i), 0)),
            pl.BlockSpec((2, 256), lambda i: (0, 0)),
        ],
        out_specs=[row_spec(256), row_spec(128), row_spec(256), row_spec(256), row_spec(256), row_spec(512),
                   row_spec(1280)],
        out_shape=[out(256, BF16), out(128, BF16), out(256, BF16), out(256, BF16), out(256, BF16), out(512, BF16),
                   out(1280, F32)],
        compiler_params=_cparams(("parallel",)),
        name="in_projection",
    )(h, mod, gains, w_in, tab_a, tab_b, qk_gain_lanes)


def _softmax_pv(q_blocks, k_refs, v_refs, lanes_of, ones_lane_of):
    tq = q_blocks[0].shape[0]
    qs = jnp.concatenate(q_blocks, axis=0)
    ss = [_dot_nt(qs, k[...]) for k in k_refs]
    m = ss[0].max(axis=-1, keepdims=True)
    for s in ss[1:]:
        m = jnp.maximum(m, s.max(axis=-1, keepdims=True))
    ps = [jnp.exp2(s - m).astype(BF16) for s in ss]
    outs = []
    for b in range(len(q_blocks)):
        o = None
        for p, v in zip(ps, v_refs):
            pv = _dot(p[tq * b:tq * (b + 1)], v[:, lanes_of(b)])
            o = pv if o is None else o + pv
        lane = ones_lane_of(b)
        outs.append(o / o[:, lane:lane + 1])
    return outs


def _attn_kernel(*refs, n_parts, lam_init):
    qa_ref = refs[0]
    ka = refs[1:1 + n_parts]
    va = refs[1 + n_parts:1 + 2 * n_parts]
    qb_ref = refs[1 + 2 * n_parts]
    kb = refs[2 + 2 * n_parts:2 + 3 * n_parts]
    vb = refs[2 + 3 * n_parts:2 + 4 * n_parts]
    lam_ref, dg_ref, ya_ref, yb_ref = refs[2 + 4 * n_parts:]

    lane128 = lax.broadcasted_iota(jnp.int32, (1, 128), 1)
    first = lane128 < HEAD_DIM
    qa = qa_ref[...]
    zero = jnp.zeros((), BF16)
    blocks = [jnp.where((lane128 // HEAD_DIM) == g, qa[:, 128 * j:128 * j + 128], zero)
              for g in range(2) for j in range(2)]
    outs = _softmax_pv(blocks, ka, va, lambda b: slice(128 * (b // 2), 128 * (b // 2) + 128),
                       lambda b: HEAD_DIM * (1 - b // 2))
    y0 = jnp.where(first, outs[0], outs[2])
    y1 = jnp.where(first, outs[1], outs[3])
    ya_ref[...] = _swap_middle_heads(jnp.concatenate([y0, y1], axis=1)).astype(BF16)

    lp = lam_ref[...]
    lam = (jnp.exp(jnp.sum(lp[0:1] * lp[1:2], axis=-1, keepdims=True))
           - jnp.exp(jnp.sum(lp[2:3] * lp[3:4], axis=-1, keepdims=True)) + lam_init)
    half = lax.broadcasted_iota(jnp.int32, (1, 256), 1) // DIFF_QK_DIM
    qb = qb_ref[...]
    blocks = [jnp.where(half == b, qb, zero) for b in range(8)]
    outs = _softmax_pv(blocks, kb, vb, lambda b: slice(128 * (b // 2), 128 * (b // 2) + 128),
                       lambda b: HEAD_DIM * (1 - (b // 2) % 2))
    d = [outs[2 * h] - lam * outs[2 * h + 1] for h in range(4)]
    y = jnp.concatenate([jnp.where(first, d[0], d[1]), jnp.where(first, d[2], d[3])], axis=1)
    ones = _block_ones(256, HEAD_DIM).astype(F32)
    yn = y * lax.rsqrt(_group_sums(y * y, ones) * (1.0 / HEAD_DIM) + EPS) * dg_ref[...]
    yb_ref[...] = (yn * (1.0 - lam_init)).astype(BF16)


def _attention(qa, ka, va, qb, kb, vb, diff_lambda, diff_gain_lanes, *, lam_init, latent):
    ctx_blk0 = T_LAT // CTX_LEN
    if latent:
        grid = (BATCH, SEQ // TQ)
        q_map = lambda b, i: (b * (SEQ // TQ) + i, 0)
        kv_specs = lambda w: [pl.BlockSpec((CTX_LEN, w), lambda b, i: (ctx_blk0 + b, 0)),
                              pl.BlockSpec((SEQ, w), lambda b, i: (b, 0))]
        rows = T_LAT
        n_parts = 2
        const = lambda b, i: (0, 0)
        o_map = q_map
    else:
        grid = (BATCH,)
        q_map = lambda b: (ctx_blk0 + b, 0)
        kv_specs = lambda w: [pl.BlockSpec((CTX_LEN, w), lambda b: (ctx_blk0 + b, 0))]
        rows = T_CTX
        n_parts = 1
        const = lambda b: (0, 0)
        o_map = lambda b: (b, 0)
    in_specs = ([pl.BlockSpec((TQ, 256), q_map)] + kv_specs(128) + kv_specs(256)
                + [pl.BlockSpec((TQ, 256), q_map)] + kv_specs(256) + kv_specs(512)
                + [pl.BlockSpec((4, DIFF_QK_DIM), const), pl.BlockSpec((1, 256), const)])
    args = [qa] + [ka] * n_parts + [va] * n_parts + [qb] + [kb] * n_parts + [vb] * n_parts
    return pl.pallas_call(
        functools.partial(_attn_kernel, n_parts=n_parts, lam_init=lam_init),
        grid=grid,
        in_specs=in_specs,
        out_specs=[pl.BlockSpec((TQ, 256), o_map), pl.BlockSpec((TQ, 256), o_map)],
        out_shape=[jax.ShapeDtypeStruct((rows, 256), BF16), jax.ShapeDtypeStruct((rows, 256), BF16)],
        compiler_params=_cparams(("parallel",) * len(grid)),
        name="attention_latent" if latent else "attention_context",
    )(*args, diff_lambda, diff_gain_lanes)


POOL_PAD = 16


def _pool(u_ref, n_seq, pw, ps, pa, pb, pc):
    pad = POOL_PAD
    n = n_seq + pad
    z = jnp.zeros((pad, 128), F32)
    first = lax.broadcasted_iota(jnp.int32, (1, 128), 1) < 64

    def shifted_sum(src, dst, k):
        dst[0:n, :] = src[0:n, :] + src[k:n + k, :]

    tiles = []
    for tile in range(2):
        pa[0:pad, :] = z
        pa[pad:n, :] = u_ref[0:n_seq, 128 * tile:128 * tile + 128]
        pa[n:n + pad, :] = z
        pb[n:n + pad, :] = z
        pc[n:n + pad, :] = z
        shifted_sum(pa, pb, 1)
        shifted_sum(pb, pc, 2)
        if tile == 0:
            w_lo = pb[pad - 1:pad - 1 + n_seq, :]
            w_hi = pc[pad - 2:pad - 2 + n_seq, :]
        else:
            shifted_sum(pc, pb, 4)
            w_lo = pb[pad - 4:pad - 4 + n_seq, :]
            shifted_sum(pb, pc, 8)
            w_hi = pc[pad - 8:pad - 8 + n_seq, :]
        tiles.append(jnp.where(first, w_lo, w_hi))
    wsum = jnp.concatenate(tiles, axis=1)
    grp = lax.broadcasted_iota(jnp.int32, (1, 256), 1) // 64
    back = jnp.where(grp == 0, 1, jnp.where(grp == 1, 2, jnp.where(grp == 2, 4, 8)))
    t = lax.broadcasted_iota(jnp.int32, (n_seq, 256), 0)
    count = jnp.minimum(t + (back - 1), n_seq - 1) - jnp.maximum(t - back, 0) + 1
    pooled = wsum / count.astype(F32) - u_ref[0:n_seq, 0:256]
    return _dot(pooled, pw) * ps


def _log_sigmoid(x):
    return jnp.minimum(x, 0.0) - jnp.log(1.0 + jnp.exp(-jnp.abs(x)))


def _mix_kernel(*refs, with_ctx):
    if with_ctx:
        (ul_ref, uc_ref, dl_ref, dh_ref, rg_ref, pw_ref, ps_ref,
         ycl_ref, ydl_ref, ycc_ref, ydc_ref, tst_ref, pa, pb, pc) = refs
    else:
        (ul_ref, uc_ref, dl_ref, dh_ref, rg_ref, pw_ref, ps_ref, ycl_ref, ydl_ref, tst_ref, pa, pb, pc) = refs
    c = RET_C
    n_chunks = SEQ // c
    pw = pw_ref[...]
    ps = ps_ref[...]

    ycl_ref[...] = _pool(ul_ref, SEQ, pw, ps, pa, pb, pc).astype(BF16)
    if with_ctx:
        ycc_ref[...] = _pool(uc_ref, CTX_LEN, pw, ps, pa, pb, pc).astype(BF16)

    lg = _log_sigmoid(dl_ref[...])
    lgh = _log_sigmoid(dh_ref[...])
    lgf, lgb = lg[0:1, :], lg[1:2, :]
    pos = lax.broadcasted_iota(jnp.int32, (c, 1), 0).astype(F32)
    wkf = jnp.exp((c - 1.0 - pos) * lgf)
    wkb = jnp.exp(pos * lgb)
    wqf = jnp.exp((pos + 1.0) * lgf)
    wqb = jnp.exp((c - pos) * lgb)
    dcf = jnp.exp(c * lgf)
    dcb = jnp.exp(c * lgb)
    same_head = _block_ones(256, HEAD_DIM)
    ones = same_head.astype(F32)
    lane = lax.broadcasted_iota(jnp.int32, (1, 256), 1) // HEAD_DIM
    diff = (lax.broadcasted_iota(jnp.int32, (c, c), 0) - lax.broadcasted_iota(jnp.int32, (c, c), 1)).astype(F32)
    decay = []
    for h in range(4):
        df = jnp.exp(jnp.maximum(diff, 0.0) * lgh[h:h + 1, 0:1])
        db = jnp.exp(jnp.maximum(-diff, 0.0) * lgh[4 + h:5 + h, 0:1])
        decay.append(jnp.where(diff > 0, df, jnp.where(diff < 0, db, 2.0)))
    rg = rg_ref[...]
    k_scale = HEAD_DIM ** -0.5

    def load(ref, r0):
        rows = pl.ds(r0, c)
        return ref[rows, 256:512], ref[rows, 512:768] * k_scale, ref[rows, 768:1024], ref[rows, 1024:1280]

    def increment(k, v, wk):
        return jnp.where(same_head, _dot_tn(k * wk, v), 0.0)

    def chunk_out(q, k, v, gate, s_fwd, s_bwd):
        o = _dot(q * wqf, s_fwd) + _dot(q * wqb, s_bwd)
        s = _dot_nt(jnp.concatenate([jnp.where(lane == h, q, 0.0) for h in range(4)], axis=0), k)
        p = jnp.concatenate([s[c * h:c * (h + 1)] * decay[h] for h in range(4)], axis=1)
        o = o + _dot(p, jnp.concatenate([jnp.where(lane == h, v, 0.0) for h in range(4)], axis=0))
        on = o * lax.rsqrt(_group_sums(o * o, ones) * (1.0 / HEAD_DIM) + EPS) * rg
        return (on * _silu(gate)).astype(BF16)

    _, kc, vc, _ = load(uc_ref, 0)
    s_fwd = increment(kc, vc, wkf)
    s_bwd = increment(kc, vc, wkb)

    for j in range(n_chunks - 1, -1, -1):
        tst_ref[j] = s_bwd
        if j > 0:
            _, k, v, _ = load(ul_ref, j * c)
            s_bwd = s_bwd * dcb + increment(k, v, wkb)

    for j in range(n_chunks):
        q, k, v, gate = load(ul_ref, j * c)
        ydl_ref[pl.ds(j * c, c), :] = chunk_out(q, k, v, gate, s_fwd, tst_ref[j])
        if j < n_chunks - 1:
            s_fwd = s_fwd * dcf + increment(k, v, wkf)

    if with_ctx:
        qc, kc, vc, gc = load(uc_ref, 0)
        zeros = jnp.zeros((256, 256), F32)
        ydc_ref[...] = chunk_out(qc, kc, vc, gc, zeros, zeros)


def _mixer(ucd, decay_lanes, decay_heads, ret_gain_lanes, pool_w_bd, pool_scale, *, with_ctx):
    ctx_blk0 = T_LAT // CTX_LEN
    const = lambda b: (0, 0)
    out_specs = [pl.BlockSpec((SEQ, 256), lambda b: (b, 0))] * 2
    out_shape = [jax.ShapeDtypeStruct((T_LAT, 256), BF16)] * 2
    if with_ctx:
        out_specs = out_specs + [pl.BlockSpec((CTX_LEN, 256), lambda b: (b, 0))] * 2
        out_shape = out_shape + [jax.ShapeDtypeStruct((T_CTX, 256), BF16)] * 2
    return pl.pallas_call(
        functools.partial(_mix_kernel, with_ctx=with_ctx),
        grid=(BATCH,),
        in_specs=[
            pl.BlockSpec((SEQ, 1280), lambda b: (b, 0)),
            pl.BlockSpec((CTX_LEN, 1280), lambda b: (ctx_blk0 + b, 0)),
            pl.BlockSpec((2, 256), const),
            pl.BlockSpec((8, 128), const),
            pl.BlockSpec((1, 256), const),
            pl.BlockSpec((256, 256), const),
            pl.BlockSpec((1, 256), const),
        ],
        out_specs=out_specs,
        out_shape=out_shape,
        scratch_shapes=[pltpu.VMEM((SEQ // RET_C, 256, 256), F32)]
        + [pltpu.VMEM((SEQ + 2 * POOL_PAD, 128), F32)] * 3,
        compiler_params=_cparams(("parallel",)),
        name="pool_retention_mixer",
    )(ucd, ucd, decay_lanes, decay_heads, ret_gain_lanes, pool_w_bd, pool_scale)


def _rope_tables(dim, tm):
    q = dim // 4
    n_rows = SEQ // GRID_W
    rows = np.repeat(np.arange(n_rows), GRID_W).astype(np.float64)
    cols = np.tile(np.arange(GRID_W), n_rows).astype(np.float64)
    inv = ROPE_BASE ** (-np.arange(q, dtype=np.float64) / q)
    ar, ac = rows[:, None] * inv, cols[:, None] * inv
    cos = np.concatenate([np.cos(ar)] * 2 + [np.cos(ac)] * 2, axis=-1)
    sin = np.concatenate([np.sin(ar)] * 2 + [np.sin(ac)] * 2, axis=-1)
    first_half = np.tile(np.repeat(np.array([True, False]), q), 2)
    tabs = np.stack([cos, np.where(first_half, -sin, 0.0), np.where(first_half, 0.0, sin)])
    tabs = np.tile(tabs, (1, 1, 128 // dim))
    ident = np.stack([np.ones((tm, 128)), np.zeros((tm, 128)), np.zeros((tm, 128))])
    return jnp.asarray(np.concatenate([tabs, ident], axis=1), dtype=F32)


def _block_diag(blocks):
    n, d, _ = blocks.shape
    eye = jnp.eye(n, dtype=blocks.dtype)
    return jnp.einsum('gcd,gh->gchd', blocks, eye).reshape(n * d, n * d)


def kernel(x, c, ctx, c_ctx, w_mod, b_mod, norm_gain, ffn_w_gate, ffn_w_up, ffn_w_down, w_in, w_out,
           attn_qk_gain, diff_lambda, diff_out_gain, pool_w, pool_scale, ret_decay_logit, ret_out_gain):
    xs = (x.reshape(T_LAT, D_MODEL), ctx.reshape(T_CTX, D_MODEL))
    cc = jnp.concatenate([c, c_ctx[None, :], jnp.zeros((MOD_ROWS - BATCH - 1, D_MODEL), F32)], axis=0)
    mod_all = _modulation(cc, w_mod, b_mod).reshape(DEPTH, MOD_ROWS, 9, D_MODEL)
    tab_a = _rope_tables(HEAD_DIM, TM_PROJ)
    tab_b = _rope_tables(DIFF_QK_DIM, TM_PROJ)

    for i in range(DEPTH):
        last = i == DEPTH - 1
        mod = mod_all[i]
        gains = norm_gain[i]
        qk_gain_lanes = jnp.tile(attn_qk_gain[i], (1, 256 // HEAD_DIM))
        diff_gain_lanes = jnp.tile(diff_out_gain[i], 256 // HEAD_DIM)[None, :]
        ret_gain_lanes = jnp.tile(ret_out_gain[i], 256 // HEAD_DIM)[None, :]
        decay_lanes = jnp.repeat(ret_decay_logit[i], HEAD_DIM, axis=1)
        decay_heads = jnp.broadcast_to(ret_decay_logit[i].reshape(8, 1), (8, 128))
        pool_w_bd = _block_diag(pool_w[i])
        lam_init = 0.8 - 0.6 * math.exp(-0.3 * i)

        def ffn(hs, sub, rows, **mix):
            return _ffn_sublayer(hs, mod, gains, ffn_w_gate, ffn_w_up, ffn_w_down, layer=i, sub=sub, rows=rows,
                                 **mix)

        h = ffn(xs, 0, T_ALL)
        qa, ka, va, qb, kb, vb, ucd = _in_projection(h, mod, gains, w_in, tab_a, tab_b, qk_gain_lanes, layer=i)
        ya, yb = _attention(qa, ka, va, qb, kb, vb, diff_lambda[i], diff_gain_lanes, lam_init=lam_init, latent=True)
        mixed = _mixer(ucd, decay_lanes, decay_heads, ret_gain_lanes, pool_w_bd, pool_scale[i][None, :],
                       with_ctx=not last)
        if last:
            ys, rows = [(ya,), (yb,), (mixed[0],), (mixed[1],)], T_LAT
        else:
            ya_c, yb_c = _attention(qa, ka, va, qb, kb, vb, diff_lambda[i], diff_gain_lanes, lam_init=lam_init,
                                    latent=False)
            ys, rows = [(ya, ya_c), (yb, yb_c), (mixed[0], mixed[2]), (mixed[1], mixed[3])], T_ALL
        h = ffn((h,), 2, rows, ys=ys, w_out=w_out)
        xs = (h,)
    return h.reshape(BATCH, SEQ, D_MODEL)
```

```python
import functools
import math

import jax
import jax.numpy as jnp
import numpy as np
from jax import lax
from jax.experimental import pallas as pl
from jax.experimental.pallas import tpu as pltpu

F32 = jnp.float32
BF16 = jnp.bfloat16

D_MODEL = 1024
BATCH = 8
SEQ = 2048
DEPTH = 2
CTX_LEN = 256
GRID_W = 64
HEAD_DIM = 64
DIFF_QK_DIM = 32
D_FF = 2816
D_IN = 2560
FFN_RESIDUAL = 0.5
ROPE_BASE = 10000.0
EPS = 1e-6

T_LAT = BATCH * SEQ
T_CTX = BATCH * CTX_LEN
T_ALL = T_LAT + T_CTX
MOD_ROWS = 16

TM_FFN = 512
FFN_SPLIT = 2
TM_PROJ = 1024
LOG2E = 1.4426950408889634
TQ = 256
RET_C = 256
VMEM_LIMIT = 56 * 1024 * 1024


def _cparams(sem):
    return pltpu.CompilerParams(dimension_semantics=sem, vmem_limit_bytes=VMEM_LIMIT)


def _silu(x):
    return x * jax.nn.sigmoid(x)


_MXU = dict(preferred_element_type=F32, precision=lax.Precision.DEFAULT)


def _dot(a, b):
    return jnp.dot(a, b, **_MXU)


def _dot_nt(a, b):
    return lax.dot_general(a, b, (((1,), (1,)), ((), ())), **_MXU)


def _dot_tn(a, b):
    return lax.dot_general(a, b, (((0,), (0,)), ((), ())), **_MXU)


def _block_ones(n, blk):
    r = lax.broadcasted_iota(jnp.int32, (n, n), 0) // blk
    c = lax.broadcasted_iota(jnp.int32, (n, n), 1) // blk
    return r == c


def _group_sums(sq, ones):
    return _dot(sq, ones)


def _rms_rows(x):
    return lax.rsqrt(jnp.mean(x * x, axis=-1, keepdims=True) + EPS)


def _mod_row_index(tile, tm):
    return jnp.minimum((tile * tm) // SEQ, BATCH)


def _mod_kernel(cc_ref, w_ref, b_ref, o_ref):
    o_ref[...] = _dot(_silu(cc_ref[...]), w_ref[...]) + b_ref[...]


def _modulation(cc, w_mod, b_mod):
    tn = 2304
    n_out = 9 * D_MODEL
    return pl.pallas_call(
        _mod_kernel,
        grid=(DEPTH, n_out // tn),
        in_specs=[
            pl.BlockSpec((MOD_ROWS, D_MODEL), lambda l, j: (0, 0)),
            pl.BlockSpec((None, D_MODEL, tn), lambda l, j: (l, 0, j)),
            pl.BlockSpec((None, 1, tn), lambda l, j: (l, 0, j)),
        ],
        out_specs=pl.BlockSpec((None, MOD_ROWS, tn), lambda l, j: (l, 0, j)),
        out_shape=jax.ShapeDtypeStruct((DEPTH, MOD_ROWS, n_out), F32),
        compiler_params=_cparams(("parallel", "parallel")),
        name="adaln_modulation",
    )(cc, w_mod, b_mod.reshape(DEPTH, 1, n_out))


def _token_tile(refs, rows, lat_tiles):
    if len(refs) == 1:
        return refs[0][rows, :]
    return jnp.where(pl.program_id(0) < lat_tiles, refs[0][rows, :], refs[1][rows, :])


def _ffn_kernel(*refs, n_x, n_y, sub, g_pre, g_post, lat_tiles):
    x_refs = refs[:n_x]
    m_ref, g_ref = refs[n_x:n_x + 2]
    y_refs = refs[n_x + 2:n_x + 2 + 4 * n_y]
    wo_ref = refs[n_x + 2 + 4 * n_y] if n_y else None
    wg_ref, wu_ref, wd_ref, o_ref = refs[-4:]
    pre = g_ref[g_pre:g_pre + 1, :] * (1.0 + m_ref[3 * sub + 1:3 * sub + 2, :])
    shift = m_ref[3 * sub:3 * sub + 1, :]
    post = (FFN_RESIDUAL * m_ref[3 * sub + 2:3 * sub + 3, :]) * g_ref[g_post:g_post + 1, :]
    mix_post = m_ref[5:6, :] * g_ref[3:4, :]
    rs = o_ref.shape[0] // FFN_SPLIT
    for s in range(FFN_SPLIT):
        rows = pl.ds(s * rs, rs)
        x = _token_tile(x_refs, rows, lat_tiles)
        if n_y:
            mixed = jnp.concatenate([_token_tile(y_refs[n_y * j:n_y * (j + 1)], rows, lat_tiles) for j in range(4)],
                                    axis=1)
            y = _dot(mixed.astype(F32), wo_ref[...])
            x = x + (y * _rms_rows(y)) * mix_post
        hm = x * _rms_rows(x) * pre + shift
        gate = _dot(hm, wg_ref[...])
        up = _dot(hm, wu_ref[...])
        y = _dot(_silu(gate) * up, wd_ref[...])
        o_ref[rows, :] = x + (y * _rms_rows(y)) * post


def _stream_specs(xs, tm, width):
    if len(xs) == 1:
        return [pl.BlockSpec((tm, width), lambda i: (i, 0))]
    lat_tiles = xs[0].shape[0] // tm
    return [pl.BlockSpec((tm, width), lambda i: (jnp.minimum(i, lat_tiles - 1), 0)),
            pl.BlockSpec((tm, width), lambda i: (jnp.maximum(i - lat_tiles, 0), 0))]


def _ffn_sublayer(xs, mod, gains, wg, wu, wd, *, layer, sub, rows, ys=(), w_out=None):
    tm = TM_FFN
    n_y = len(ys[0]) if ys else 0
    kern = functools.partial(_ffn_kernel, n_x=len(xs), n_y=n_y, sub=sub, g_pre=2 * sub, g_post=2 * sub + 1,
                             lat_tiles=T_LAT // tm)
    k = sub // 2
    resident = dict(pipeline_mode=pl.Buffered(1))
    y_specs, y_args = [], []
    for parts in ys:
        y_specs += _stream_specs(parts, tm, 256)
        y_args += list(parts)
    if ys:
        y_specs.append(pl.BlockSpec((None, D_MODEL, D_MODEL), lambda i: (layer, 0, 0), **resident))
        y_args.append(w_out)
    return pl.pallas_call(
        kern,
        grid=(rows // tm,),
        in_specs=_stream_specs(xs, tm, D_MODEL) + [
            pl.BlockSpec((None, 9, D_MODEL), lambda i: (_mod_row_index(i, tm), 0, 0)),
            pl.BlockSpec((6, D_MODEL), lambda i: (0, 0)),
        ] + y_specs + [
            pl.BlockSpec((None, None, D_MODEL, D_FF), lambda i: (layer, k, 0, 0), **resident),
            pl.BlockSpec((None, None, D_MODEL, D_FF), lambda i: (layer, k, 0, 0), **resident),
            pl.BlockSpec((None, None, D_FF, D_MODEL), lambda i: (layer, k, 0, 0), **resident),
        ],
        out_specs=pl.BlockSpec((tm, D_MODEL), lambda i: (i, 0)),
        out_shape=jax.ShapeDtypeStruct((rows, D_MODEL), F32),
        compiler_params=_cparams(("parallel",)),
        name=f"ffn_sublayer_{sub}",
    )(*xs, mod, gains, *y_args, wg, wu, wd)


def _rope(x, tab_ref, rows, quarter):
    w = x.shape[1]
    reps = w // tab_ref.shape[2]

    def tab(k):
        t = tab_ref[k, rows, :]
        return t if reps == 1 else jnp.concatenate([t] * reps, axis=1)

    return x * tab(0) + pltpu.roll(x, w - quarter, 1) * tab(1) + pltpu.roll(x, quarter, 1) * tab(2)


def _swap_middle_heads(x):
    first = lax.broadcasted_iota(jnp.int32, (1, 128), 1) < HEAD_DIM
    lo, hi = x[:, 0:128], x[:, 128:256]
    return jnp.concatenate([jnp.where(first, lo, pltpu.roll(hi, HEAD_DIM, 1)),
                            jnp.where(first, pltpu.roll(lo, HEAD_DIM, 1), hi)], axis=1)


def _inproj_kernel(x_ref, m_ref, g_ref, w_ref, ta_ref, tb_ref, qkg_ref,
                   qa_ref, ka_ref, va_ref, qb_ref, kb_ref, vb_ref, ucd_ref):
    pre = g_ref[2:3, :] * (1.0 + m_ref[4:5, :])
    shift = m_ref[3:4, :]
    ones = _block_ones(256, HEAD_DIM).astype(F32)
    inv_d = 1.0 / HEAD_DIM
    lane128 = lax.broadcasted_iota(jnp.int32, (1, 128), 1)
    rows = pl.ds(0, x_ref.shape[0])
    x = x_ref[...]
    hm = x * _rms_rows(x) * pre + shift
    u_ab = _dot(hm, w_ref[:, 0:1280])

    def proj(lo, hi):
        return u_ab[:, lo:hi]

    q = proj(0, 256)
    q = q * lax.rsqrt(_group_sums(q * q, ones) * inv_d + EPS) * qkg_ref[0:1, :]
    q = _rope(q, ta_ref, rows, HEAD_DIM // 4) * (HEAD_DIM ** -0.5 * LOG2E)
    qa_ref[...] = _swap_middle_heads(q).astype(BF16)
    k = proj(256, 384)
    k = k * lax.rsqrt(_group_sums(k * k, ones[:128, :128]) * inv_d + EPS) * qkg_ref[1:2, :128]
    ka_ref[...] = _rope(k, ta_ref, rows, HEAD_DIM // 4).astype(BF16)
    v = proj(384, 512)
    for g in range(2):
        va_ref[:, 128 * g:128 * g + 128] = jnp.where(lane128 == HEAD_DIM * (1 - g), 1.0, v).astype(BF16)

    qb_ref[...] = (_rope(proj(512, 768), tb_ref, rows, DIFF_QK_DIM // 4)
                   * (DIFF_QK_DIM ** -0.5 * LOG2E)).astype(BF16)
    kb_ref[...] = _rope(proj(768, 1024), tb_ref, rows, DIFF_QK_DIM // 4).astype(BF16)
    v = proj(1024, 1280)
    for hd in range(4):
        own = (lane128 // HEAD_DIM) == hd % 2
        ones_col = jnp.where(lane128 == HEAD_DIM * (1 - hd % 2), 1.0, 0.0)
        pair = v[:, 128 * (hd // 2):128 * (hd // 2) + 128]
        vb_ref[:, 128 * hd:128 * hd + 128] = jnp.where(own, pair, ones_col).astype(BF16)

    ucd_ref[...] = _dot(hm, w_ref[:, 1280:2560])


def _in_projection(h, mod, gains, w_in, tab_a, tab_b, qk_gain_lanes, *, layer):
    tm = TM_PROJ
    lat_tiles = T_LAT // tm
    tiles_per_seq = SEQ // tm

    def tab_idx(i):
        return jnp.where(i < lat_tiles, i % tiles_per_seq, tiles_per_seq)

    def row_spec(width):
        return pl.BlockSpec((tm, width), lambda i: (i, 0))

    def out(width, dtype):
        return jax.ShapeDtypeStruct((T_ALL, width), dtype)

    return pl.pallas_call(
        _inproj_kernel,
        grid=(T_ALL // tm,),
        in_specs=[
            row_spec(D_MODEL),
            pl.BlockSpec((None, 9, D_MODEL), lambda i: (_mod_row_index(i, tm), 0, 0)),
            pl.BlockSpec((6, D_MODEL), lambda i: (0, 0)),
            pl.BlockSpec((None, D_MODEL, D_IN), lambda i: (layer, 0, 0), pipeline_mode=pl.Buffered(1)),
            pl.BlockSpec((3, tm, 128), lambda i: (0, tab_idx(i), 0)),
            pl.BlockSpec((3, tm, 128), lambda i: (0, tab_idx(i), 0)),
            pl.BlockSpec((2, 256), lambda i: (0, 0)),
        ],
        out_specs=[row_spec(256), row_spec(128), row_spec(256), row_spec(256), row_spec(256), row_spec(512),
                   row_spec(1280)],
        out_shape=[out(256, BF16), out(128, BF16), out(256, BF16), out(256, BF16), out(256, BF16), out(512, BF16),
                   out(1280, F32)],
        compiler_params=_cparams(("parallel",)),
        name="in_projection",
    )(h, mod, gains, w_in, tab_a, tab_b, qk_gain_lanes)


def _softmax_pv(q_blocks, k_refs, v_refs, lanes_of, ones_lane_of):
    tq = q_blocks[0].shape[0]
    qs = jnp.concatenate(q_blocks, axis=0)
    ss = [_dot_nt(qs, k[...]) for k in k_refs]
    m = ss[0].max(axis=-1, keepdims=True)
    for s in ss[1:]:
        m = jnp.maximum(m, s.max(axis=-1, keepdims=True))
    ps = [jnp.exp2(s - m).astype(BF16) for s in ss]
    outs = []
    for b in range(len(q_blocks)):
        o = None
        for p, v in zip(ps, v_refs):
            pv = _dot(p[tq * b:tq * (b + 1)], v[:, lanes_of(b)])
            o = pv if o is None else o + pv
        lane = ones_lane_of(b)
        outs.append(o / o[:, lane:lane + 1])
    return outs


def _attn_kernel(*refs, n_parts, lam_init):
    qa_ref = refs[0]
    ka = refs[1:1 + n_parts]
    va = refs[1 + n_parts:1 + 2 * n_parts]
    qb_ref = refs[1 + 2 * n_parts]
    kb = refs[2 + 2 * n_parts:2 + 3 * n_parts]
    vb = refs[2 + 3 * n_parts:2 + 4 * n_parts]
    lam_ref, dg_ref, ya_ref, yb_ref = refs[2 + 4 * n_parts:6 + 4 * n_parts]
    if n_parts == 2:
        merged = refs[6 + 4 * n_parts:]

        @pl.when(pl.program_id(1) == 0)
        def _():
            for dst, parts in zip(merged, (ka, va, kb, vb)):
                dst[0:CTX_LEN, :] = parts[0][...]
                dst[CTX_LEN:CTX_LEN + SEQ, :] = parts[1][...]

        ka, va, kb, vb = ([r] for r in merged)

    lane128 = lax.broadcasted_iota(jnp.int32, (1, 128), 1)
    first = lane128 < HEAD_DIM
    qa = qa_ref[...]
    zero = jnp.zeros((), BF16)
    blocks = [jnp.where((lane128 // HEAD_DIM) == g, qa[:, 128 * j:128 * j + 128], zero)
              for g in range(2) for j in range(2)]
    outs = _softmax_pv(blocks, ka, va, lambda b: slice(128 * (b // 2), 128 * (b // 2) + 128),
                       lambda b: HEAD_DIM * (1 - b // 2))
    y0 = jnp.where(first, outs[0], outs[2])
    y1 = jnp.where(first, outs[1], outs[3])
    ya_ref[...] = _swap_middle_heads(jnp.concatenate([y0, y1], axis=1)).astype(BF16)

    lp = lam_ref[...]
    lam = (jnp.exp(jnp.sum(lp[0:1] * lp[1:2], axis=-1, keepdims=True))
           - jnp.exp(jnp.sum(lp[2:3] * lp[3:4], axis=-1, keepdims=True)) + lam_init)
    half = lax.broadcasted_iota(jnp.int32, (1, 256), 1) // DIFF_QK_DIM
    qb = qb_ref[...]
    blocks = [jnp.where(half == b, qb, zero) for b in range(8)]
    outs = _softmax_pv(blocks, kb, vb, lambda b: slice(128 * (b // 2), 128 * (b // 2) + 128),
                       lambda b: HEAD_DIM * (1 - (b // 2) % 2))
    d = [outs[2 * h] - lam * outs[2 * h + 1] for h in range(4)]
    y = jnp.concatenate([jnp.where(first, d[0], d[1]), jnp.where(first, d[2], d[3])], axis=1)
    ones = _block_ones(256, HEAD_DIM).astype(F32)
    yn = y * lax.rsqrt(_group_sums(y * y, ones) * (1.0 / HEAD_DIM) + EPS) * dg_ref[...]
    yb_ref[...] = (yn * (1.0 - lam_init)).astype(BF16)


def _attention(qa, ka, va, qb, kb, vb, diff_lambda, diff_gain_lanes, *, lam_init, latent):
    ctx_blk0 = T_LAT // CTX_LEN
    if latent:
        grid = (BATCH, SEQ // TQ)
        q_map = lambda b, i: (b * (SEQ // TQ) + i, 0)
        kv_specs = lambda w: [pl.BlockSpec((CTX_LEN, w), lambda b, i: (ctx_blk0 + b, 0)),
                              pl.BlockSpec((SEQ, w), lambda b, i: (b, 0))]
        rows = T_LAT
        n_parts = 2
        const = lambda b, i: (0, 0)
        o_map = q_map
    else:
        grid = (BATCH,)
        q_map = lambda b: (ctx_blk0 + b, 0)
        kv_specs = lambda w: [pl.BlockSpec((CTX_LEN, w), lambda b: (ctx_blk0 + b, 0))]
        rows = T_CTX
        n_parts = 1
        const = lambda b: (0, 0)
        o_map = lambda b: (b, 0)
    in_specs = ([pl.BlockSpec((TQ, 256), q_map)] + kv_specs(128) + kv_specs(256)
                + [pl.BlockSpec((TQ, 256), q_map)] + kv_specs(256) + kv_specs(512)
                + [pl.BlockSpec((4, DIFF_QK_DIM), const), pl.BlockSpec((1, 256), const)])
    args = [qa] + [ka] * n_parts + [va] * n_parts + [qb] + [kb] * n_parts + [vb] * n_parts
    return pl.pallas_call(
        functools.partial(_attn_kernel, n_parts=n_parts, lam_init=lam_init),
        grid=grid,
        in_specs=in_specs,
        out_specs=[pl.BlockSpec((TQ, 256), o_map), pl.BlockSpec((TQ, 256), o_map)],
        out_shape=[jax.ShapeDtypeStruct((rows, 256), BF16), jax.ShapeDtypeStruct((rows, 256), BF16)],
        scratch_shapes=[pltpu.VMEM((CTX_LEN + SEQ, w), BF16) for w in (128, 256, 256, 512)] if latent else [],
        compiler_params=_cparams(("parallel", "arbitrary") if latent else ("parallel",)),
        name="attention_latent" if latent else "attention_context",
    )(*args, diff_lambda, diff_gain_lanes)


POOL_PAD = 16


def _pool(u_ref, n_seq, pw, ps, pa, pb, pc):
    pad = POOL_PAD
    n = n_seq + pad
    z = jnp.zeros((pad, 128), F32)
    first = lax.broadcasted_iota(jnp.int32, (1, 128), 1) < 64

    def shifted_sum(src, dst, k):
        dst[0:n, :] = src[0:n, :] + src[k:n + k, :]

    tiles = []
    for tile in range(2):
        pa[0:pad, :] = z
        pa[pad:n, :] = u_ref[0:n_seq, 128 * tile:128 * tile + 128]
        pa[n:n + pad, :] = z
        pb[n:n + pad, :] = z
        pc[n:n + pad, :] = z
        shifted_sum(pa, pb, 1)
        shifted_sum(pb, pc, 2)
        if tile == 0:
            w_lo = pb[pad - 1:pad - 1 + n_seq, :]
            w_hi = pc[pad - 2:pad - 2 + n_seq, :]
        else:
            shifted_sum(pc, pb, 4)
            w_lo = pb[pad - 4:pad - 4 + n_seq, :]
            shifted_sum(pb, pc, 8)
            w_hi = pc[pad - 8:pad - 8 + n_seq, :]
        tiles.append(jnp.where(first, w_lo, w_hi))
    wsum = jnp.concatenate(tiles, axis=1)
    grp = lax.broadcasted_iota(jnp.int32, (1, 256), 1) // 64
    back = jnp.where(grp == 0, 1, jnp.where(grp == 1, 2, jnp.where(grp == 2, 4, 8)))
    t = lax.broadcasted_iota(jnp.int32, (n_seq, 256), 0)
    count = jnp.minimum(t + (back - 1), n_seq - 1) - jnp.maximum(t - back, 0) + 1
    pooled = wsum / count.astype(F32) - u_ref[0:n_seq, 0:256]
    return _dot(pooled, pw) * ps


def _log_sigmoid(x):
    return jnp.minimum(x, 0.0) - jnp.log(1.0 + jnp.exp(-jnp.abs(x)))


def _mix_kernel(*refs, with_ctx):
    if with_ctx:
        (ul_ref, uc_ref, dl_ref, dh_ref, rg_ref, pw_ref, ps_ref,
         ycl_ref, ydl_ref, ycc_ref, ydc_ref, tst_ref, pa, pb, pc) = refs
    else:
        (ul_ref, uc_ref, dl_ref, dh_ref, rg_ref, pw_ref, ps_ref, ycl_ref, ydl_ref, tst_ref, pa, pb, pc) = refs
    c = RET_C
    n_chunks = SEQ // c
    pw = pw_ref[...]
    ps = ps_ref[...]

    ycl_ref[...] = _pool(ul_ref, SEQ, pw, ps, pa, pb, pc).astype(BF16)
    if with_ctx:
        ycc_ref[...] = _pool(uc_ref, CTX_LEN, pw, ps, pa, pb, pc).astype(BF16)

    lg = _log_sigmoid(dl_ref[...])
    lgh = _log_sigmoid(dh_ref[...])
    lgf, lgb = lg[0:1, :], lg[1:2, :]
    pos = lax.broadcasted_iota(jnp.int32, (c, 1), 0).astype(F32)
    wkf = jnp.exp((c - 1.0 - pos) * lgf)
    wkb = jnp.exp(pos * lgb)
    wqf = jnp.exp((pos + 1.0) * lgf)
    wqb = jnp.exp((c - pos) * lgb)
    dcf = jnp.exp(c * lgf)
    dcb = jnp.exp(c * lgb)
    same_head = _block_ones(256, HEAD_DIM)
    ones = same_head.astype(F32)
    lane = lax.broadcasted_iota(jnp.int32, (1, 256), 1) // HEAD_DIM
    diff = (lax.broadcasted_iota(jnp.int32, (c, c), 0) - lax.broadcasted_iota(jnp.int32, (c, c), 1)).astype(F32)
    decay = []
    for h in range(4):
        df = jnp.exp(jnp.maximum(diff, 0.0) * lgh[h:h + 1, 0:1])
        db = jnp.exp(jnp.maximum(-diff, 0.0) * lgh[4 + h:5 + h, 0:1])
        decay.append(jnp.where(diff > 0, df, jnp.where(diff < 0, db, 2.0)))
    rg = rg_ref[...]
    k_scale = HEAD_DIM ** -0.5

    def load(ref, r0):
        rows = pl.ds(r0, c)
        return ref[rows, 256:512], ref[rows, 512:768] * k_scale, ref[rows, 768:1024], ref[rows, 1024:1280]

    def increment(k, v, wk):
        return jnp.where(same_head, _dot_tn(k * wk, v), 0.0)

    def chunk_out(q, k, v, gate, s_fwd, s_bwd):
        o = _dot(q * wqf, s_fwd) + _dot(q * wqb, s_bwd)
        s = _dot_nt(jnp.concatenate([jnp.where(lane == h, q, 0.0) for h in range(4)], axis=0), k)
        p = jnp.concatenate([s[c * h:c * (h + 1)] * decay[h] for h in range(4)], axis=1)
        o = o + _dot(p, jnp.concatenate([jnp.where(lane == h, v, 0.0) for h in range(4)], axis=0))
        on = o * lax.rsqrt(_group_sums(o * o, ones) * (1.0 / HEAD_DIM) + EPS) * rg
        return (on * _silu(gate)).astype(BF16)

    _, kc, vc, _ = load(uc_ref, 0)
    s_fwd = increment(kc, vc, wkf)
    s_bwd = increment(kc, vc, wkb)

    for j in range(n_chunks - 1, -1, -1):
        tst_ref[j] = s_bwd
        if j > 0:
            _, k, v, _ = load(ul_ref, j * c)
            s_bwd = s_bwd * dcb + increment(k, v, wkb)

    for j in range(n_chunks):
        q, k, v, gate = load(ul_ref, j * c)
        ydl_ref[pl.ds(j * c, c), :] = chunk_out(q, k, v, gate, s_fwd, tst_ref[j])
        if j < n_chunks - 1:
            s_fwd = s_fwd * dcf + increment(k, v, wkf)

    if with_ctx:
        qc, kc, vc, gc = load(uc_ref, 0)
        zeros = jnp.zeros((256, 256), F32)
        ydc_ref[...] = chunk_out(qc, kc, vc, gc, zeros, zeros)


def _mixer(ucd, decay_lanes, decay_heads, ret_gain_lanes, pool_w_bd, pool_scale, *, with_ctx):
    ctx_blk0 = T_LAT // CTX_LEN
    const = lambda b: (0, 0)
    out_specs = [pl.BlockSpec((SEQ, 256), lambda b: (b, 0))] * 2
    out_shape = [jax.ShapeDtypeStruct((T_LAT, 256), BF16)] * 2
    if with_ctx:
        out_specs = out_specs + [pl.BlockSpec((CTX_LEN, 256), lambda b: (b, 0))] * 2
        out_shape = out_shape + [jax.ShapeDtypeStruct((T_CTX, 256), BF16)] * 2
    return pl.pallas_call(
        functools.partial(_mix_kernel, with_ctx=with_ctx),
        grid=(BATCH,),
        in_specs=[
            pl.BlockSpec((SEQ, 1280), lambda b: (b, 0)),
            pl.BlockSpec((CTX_LEN, 1280), lambda b: (ctx_blk0 + b, 0)),
            pl.BlockSpec((2, 256), const),
            pl.BlockSpec((8, 128), const),
            pl.BlockSpec((1, 256), const),
            pl.BlockSpec((256, 256), const),
            pl.BlockSpec((1, 256), const),
        ],
        out_specs=out_specs,
        out_shape=out_shape,
        scratch_shapes=[pltpu.VMEM((SEQ // RET_C, 256, 256), F32)]
        + [pltpu.VMEM((SEQ + 2 * POOL_PAD, 128), F32)] * 3,
        compiler_params=_cparams(("parallel",)),
        name="pool_retention_mixer",
    )(ucd, ucd, decay_lanes, decay_heads, ret_gain_lanes, pool_w_bd, pool_scale)


def _rope_tables(dim, tm):
    q = dim // 4
    n_rows = SEQ // GRID_W
    rows = np.repeat(np.arange(n_rows), GRID_W).astype(np.float64)
    cols = np.tile(np.arange(GRID_W), n_rows).astype(np.float64)
    inv = ROPE_BASE ** (-np.arange(q, dtype=np.float64) / q)
    ar, ac = rows[:, None] * inv, cols[:, None] * inv
    cos = np.concatenate([np.cos(ar)] * 2 + [np.cos(ac)] * 2, axis=-1)
    sin = np.concatenate([np.sin(ar)] * 2 + [np.sin(ac)] * 2, axis=-1)
    first_half = np.tile(np.repeat(np.array([True, False]), q), 2)
    tabs = np.stack([cos, np.where(first_half, -sin, 0.0), np.where(first_half, 0.0, sin)])
    tabs = np.tile(tabs, (1, 1, 128 // dim))
    ident = np.stack([np.ones((tm, 128)), np.zeros((tm, 128)), np.zeros((tm, 128))])
    return jnp.asarray(np.concatenate([tabs, ident], axis=1), dtype=F32)


def _block_diag(blocks):
    n, d, _ = blocks.shape
    eye = jnp.eye(n, dtype=blocks.dtype)
    return jnp.einsum('gcd,gh->gchd', blocks, eye).reshape(n * d, n * d)


def kernel(x, c, ctx, c_ctx, w_mod, b_mod, norm_gain, ffn_w_gate, ffn_w_up, ffn_w_down, w_in, w_out,
           attn_qk_gain, diff_lambda, diff_out_gain, pool_w, pool_scale, ret_decay_logit, ret_out_gain):
    xs = (x.reshape(T_LAT, D_MODEL), ctx.reshape(T_CTX, D_MODEL))
    cc = jnp.concatenate([c, c_ctx[None, :], jnp.zeros((MOD_ROWS - BATCH - 1, D_MODEL), F32)], axis=0)
    mod_all = _modulation(cc, w_mod, b_mod).reshape(DEPTH, MOD_ROWS, 9, D_MODEL)
    tab_a = _rope_tables(HEAD_DIM, TM_PROJ)
    tab_b = _rope_tables(DIFF_QK_DIM, TM_PROJ)

    for i in range(DEPTH):
        last = i == DEPTH - 1
        mod = mod_all[i]
        gains = norm_gain[i]
        qk_gain_lanes = jnp.tile(attn_qk_gain[i], (1, 256 // HEAD_DIM))
        diff_gain_lanes = jnp.tile(diff_out_gain[i], 256 // HEAD_DIM)[None, :]
        ret_gain_lanes = jnp.tile(ret_out_gain[i], 256 // HEAD_DIM)[None, :]
        decay_lanes = jnp.repeat(ret_decay_logit[i], HEAD_DIM, axis=1)
        decay_heads = jnp.broadcast_to(ret_decay_logit[i].reshape(8, 1), (8, 128))
        pool_w_bd = _block_diag(pool_w[i])
        lam_init = 0.8 - 0.6 * math.exp(-0.3 * i)

        def ffn(hs, sub, rows, **mix):
            return _ffn_sublayer(hs, mod, gains, ffn_w_gate, ffn_w_up, ffn_w_down, layer=i, sub=sub, rows=rows,
                                 **mix)

        h = ffn(xs, 0, T_ALL)
        qa, ka, va, qb, kb, vb, ucd = _in_projection(h, mod, gains, w_in, tab_a, tab_b, qk_gain_lanes, layer=i)
        ya, yb = _attention(qa, ka, va, qb, kb, vb, diff_lambda[i], diff_gain_lanes, lam_init=lam_init, latent=True)
        mixed = _mixer(ucd, decay_lanes, decay_heads, ret_gain_lanes, pool_w_bd, pool_scale[i][None, :],
                       with_ctx=not last)
        if last:
            ys, rows = [(ya,), (yb,), (mixed[0],), (mixed[1],)], T_LAT
        else:
            ya_c, yb_c = _attention(qa, ka, va, qb, kb, vb, diff_lambda[i], diff_gain_lanes, lam_init=lam_init,
                                    latent=False)
            ys, rows = [(ya, ya_c), (yb, yb_c), (mixed[0], mixed[2]), (mixed[1], mixed[3])], T_ALL
        h = ffn((h,), 2, rows, ys=ys, w_out=w_out)
        xs = (h,)
    return h.reshape(BATCH, SEQ, D_MODEL)
```
